```python
import math
import jax, jax.numpy as jnp
from jax import lax
import numpy as np

D_MODEL = 1024
BATCH = 2
SEQ = 8192
DEPTH = 2

N_MIXERS = 2
N_RET_LAYERS = (DEPTH + N_MIXERS - 1) // N_MIXERS
N_S5_LAYERS = DEPTH // N_MIXERS
N_MOD = 9
D_FF = 2816
RET_HEADS = 4
RET_QK_DIM = D_MODEL // RET_HEADS
RET_V_DIM = 2 * D_MODEL // RET_HEADS
RET_CHUNK = 128
ROPE_THETA = 10000.0
S5_GROUP = 16
S5_GROUPS = D_MODEL // S5_GROUP
S5_STATE = 64
DT_MIN = 0.001
DT_MAX = 0.1
NORM_EPS = 1e-6
GN_EPS = 1e-5

kernel_name = "hybrid_retention_s5_macaron_adaln"


def rms_norm(x, g):
    xf = x.astype(jnp.float32)
    y = xf * lax.rsqrt(jnp.mean(xf * xf, axis=-1, keepdims=True) + NORM_EPS)
    return (y * g.astype(jnp.float32)).astype(x.dtype)


def modulate(h, shift, scale):
    return h * (1.0 + scale[:, None, :]) + shift[:, None, :]


def swiglu_ffn(h, w_in, w_out):
    gate, up = jnp.split(h @ w_in, 2, axis=-1)
    return (jax.nn.silu(gate) * up) @ w_out


def rotary(t, positions):
    half = t.shape[-1] // 2
    inv_freq = ROPE_THETA ** (-jnp.arange(half, dtype=jnp.float32) / half)
    ang = positions.astype(jnp.float32)[..., None] * inv_freq
    cos = jnp.cos(ang)[:, :, None, :]
    sin = jnp.sin(ang)[:, :, None, :]
    t1, t2 = t[..., :half], t[..., half:]
    return jnp.concatenate([t1 * cos - t2 * sin, t1 * sin + t2 * cos], axis=-1)


def retention(h, positions, w_in, w_out):
    B, L, _ = h.shape
    H, dk, dv, C = RET_HEADS, RET_QK_DIM, RET_V_DIM, RET_CHUNK
    N = L // C
    qk_w, v_w = H * dk, H * dv
    q, k, v, g = jnp.split(h @ w_in, [qk_w, 2 * qk_w, 2 * qk_w + v_w], axis=-1)
    q = rotary(q.reshape(B, L, H, dk).astype(jnp.float32), positions)
    k = rotary(k.reshape(B, L, H, dk).astype(jnp.float32), positions) * (dk ** -0.5)
    v = v.reshape(B, L, H, dv).astype(jnp.float32)

    log_gamma = jnp.log(1.0 - jnp.power(2.0, -5.0 - jnp.arange(H, dtype=jnp.float32)))
    pos = jnp.arange(C, dtype=jnp.float32)
    diff = pos[:, None] - pos[None, :]
    decay_mask = jnp.where(diff >= 0,
                           jnp.exp(log_gamma[:, None, None] * jnp.maximum(diff, 0.0)),
                           0.0)
    q_decay = jnp.exp(log_gamma[:, None] * (pos + 1.0))[None, :, :, None]
    k_decay = jnp.exp(log_gamma[:, None] * (C - 1.0 - pos))[None, :, :, None]
    chunk_decay = jnp.exp(log_gamma * C)[None, :, None, None]

    def to_chunks(t):
        return t.reshape(B, N, C, H, -1).transpose(1, 0, 3, 2, 4)

    def step(state, qkv):
        qc, kc, vc = qkv
        scores = jnp.einsum('bhid,bhjd->bhij', qc, kc) * decay_mask
        inner = jnp.einsum('bhij,bhje->bhie', scores, vc)
        cross = jnp.einsum('bhid,bhde->bhie', qc, state) * q_decay
        state = state * chunk_decay + jnp.einsum('bhjd,bhje->bhde', kc * k_decay, vc)
        return state, inner + cross

    state0 = jnp.zeros((B, H, dk, dv), jnp.float32)
    _, o = lax.scan(step, state0, (to_chunks(q), to_chunks(k), to_chunks(v)))
    o = o.transpose(1, 0, 3, 2, 4).reshape(B, L, H, dv)
    mu = jnp.mean(o, axis=-1, keepdims=True)
    var = jnp.mean(jnp.square(o - mu), axis=-1, keepdims=True)
    o = ((o - mu) * lax.rsqrt(var + GN_EPS)).reshape(B, L, v_w).astype(h.dtype)
    return (jax.nn.silu(g) * o) @ w_out


def s5_layer(u, a_re, a_im, b_re, b_im, c_re, c_im, d_skip, log_dt, w_glu):
    Bsz, L, Dm = u.shape
    G, P, K = S5_GROUPS, S5_STATE, S5_GROUP
    f32 = jnp.float32
    A = lax.complex(a_re.astype(f32), a_im.astype(f32))
    dt = jnp.exp(log_dt.astype(f32))[:, None]
    A_bar = jnp.exp(dt * A)
    B_mat = lax.complex(b_re.astype(f32), b_im.astype(f32))
    B_bar = ((A_bar - 1.0) / A)[..., None] * B_mat
    C_mat = lax.complex(c_re.astype(f32), c_im.astype(f32))

    uf = u.astype(f32)
    ug = uf.reshape(Bsz, L, G, K).astype(jnp.complex64)
    bu = jnp.einsum('blgk,gpk->blgp', ug, B_bar)
    a = jnp.broadcast_to(A_bar, (1, L, G, P))

    def combine(left, right):
        a_l, b_l = left
        a_r, b_r = right
        return a_l * a_r, a_r * b_l + b_r

    _, states = lax.associative_scan(combine, (a, bu), axis=1)
    y = jnp.einsum('gkp,blgp->blgk', C_mat, states).real.reshape(Bsz, L, Dm)
    y = jax.nn.gelu(y + d_skip.astype(f32) * uf).astype(u.dtype)
    y1, y2 = jnp.split(y @ w_glu, 2, axis=-1)
    return y1 * jax.nn.sigmoid(y2)


def setup_inputs(seed: int = 0) -> dict:
    key = jax.random.key(seed)
    ks = jax.random.split(key, 20)
    D, f32 = D_MODEL, jnp.float32
    G, P, K = S5_GROUPS, S5_STATE, S5_GROUP
    nrm = lambda k, shape, s: jax.random.normal(k, shape, f32) * s
    x = nrm(ks[0], (BATCH, SEQ, D), 1.0)
    c = nrm(ks[1], (BATCH, D), 1.0)
    positions = jnp.broadcast_to(jnp.arange(SEQ, dtype=jnp.int32), (BATCH, SEQ))
    ada_w = nrm(ks[2], (DEPTH, D, N_MOD * D), 0.5 * D ** -0.5)
    ada_b = nrm(ks[3], (DEPTH, N_MOD * D), 0.01)
    norm_g = 1.0 + nrm(ks[4], (DEPTH, 3, D), 0.02)
    ffn_w_in = nrm(ks[5], (DEPTH, 2, D, 2 * D_FF), D ** -0.5)
    ffn_w_out = nrm(ks[6], (DEPTH, 2, D_FF, D), D_FF ** -0.5)
    ret_in_w = 2 * RET_HEADS * RET_QK_DIM + 2 * RET_HEADS * RET_V_DIM
    ret_w_in = nrm(ks[7], (N_RET_LAYERS, D, ret_in_w), D ** -0.5)
    ret_w_out = nrm(ks[8], (N_RET_LAYERS, RET_HEADS * RET_V_DIM, D), (RET_HEADS * RET_V_DIM) ** -0.5)
    s5_a_re = -0.5 + nrm(ks[9], (N_S5_LAYERS, G, P), 0.01)
    s5_a_im = jnp.broadcast_to(math.pi * jnp.arange(P, dtype=f32), (N_S5_LAYERS, G, P))
    s5_b_re = nrm(ks[10], (N_S5_LAYERS, G, P, K), (2 * K) ** -0.5)
    s5_b_im = nrm(ks[11], (N_S5_LAYERS, G, P, K), (2 * K) ** -0.5)
    s5_c_re = nrm(ks[12], (N_S5_LAYERS, G, K, P), (2 * P) ** -0.5)
    s5_c_im = nrm(ks[13], (N_S5_LAYERS, G, K, P), (2 * P) ** -0.5)
    s5_d = nrm(ks[14], (N_S5_LAYERS, D), 1.0)
    s5_log_dt = jax.random.uniform(ks[15], (N_S5_LAYERS, G), f32,
                                   math.log(DT_MIN), math.log(DT_MAX))
    s5_w_glu = nrm(ks[16], (N_S5_LAYERS, D, 2 * D), D ** -0.5)
    final_g = 1.0 + nrm(ks[17], (D,), 0.02)
    return {"x": x, "c": c, "positions": positions,
            "ada_w": ada_w, "ada_b": ada_b, "norm_g": norm_g,
            "ffn_w_in": ffn_w_in, "ffn_w_out": ffn_w_out,
            "ret_w_in": ret_w_in, "ret_w_out": ret_w_out,
            "s5_a_re": s5_a_re, "s5_a_im": s5_a_im,
            "s5_b_re": s5_b_re, "s5_b_im": s5_b_im,
            "s5_c_re": s5_c_re, "s5_c_im": s5_c_im,
            "s5_d": s5_d, "s5_log_dt": s5_log_dt, "s5_w_glu": s5_w_glu,
            "final_g": final_g}


def reference(x, c, positions, ada_w, ada_b, norm_g, ffn_w_in, ffn_w_out,
              ret_w_in, ret_w_out, s5_a_re, s5_a_im, s5_b_re, s5_b_im,
              s5_c_re, s5_c_im, s5_d, s5_log_dt, s5_w_glu, final_g):
    c_act = jax.nn.silu(c)
    for i in range(DEPTH):
        mods = jnp.split(c_act @ ada_w[i] + ada_b[i], N_MOD, axis=-1)
        sh1, sc1, g1, sh2, sc2, g2, sh3, sc3, g3 = mods
        h = modulate(rms_norm(x, norm_g[i, 0]), sh1, sc1)
        x = x + 0.5 * g1[:, None, :] * swiglu_ffn(h, ffn_w_in[i, 0], ffn_w_out[i, 0])
        h = modulate(rms_norm(x, norm_g[i, 1]), sh2, sc2)
        j = i // N_MIXERS
        if i % N_MIXERS == 0:
            m = retention(h, positions, ret_w_in[j], ret_w_out[j])
        else:
            m = s5_layer(h, s5_a_re[j], s5_a_im[j], s5_b_re[j], s5_b_im[j],
                         s5_c_re[j], s5_c_im[j], s5_d[j], s5_log_dt[j], s5_w_glu[j])
        x = x + g2[:, None, :] * m
        h = modulate(rms_norm(x, norm_g[i, 2]), sh3, sc3)
        x = x + 0.5 * g3[:, None, :] * swiglu_ffn(h, ffn_w_in[i, 1], ffn_w_out[i, 1])
    return rms_norm(x, final_g)
```

```python
import functools

import jax
import jax.numpy as jnp
from jax import lax
from jax.experimental import pallas as pl
from jax.experimental.pallas import tpu as pltpu

F32 = jnp.float32
BF16 = jnp.bfloat16

D_MODEL = 1024
N_MOD = 9
D_FF = 2816
RET_HEADS = 4
RET_QK_DIM = 256
RET_V_DIM = 512
RET_CHUNK = 128
ROPE_THETA = 10000.0
S5_GROUP = 16
S5_GROUPS = 64
S5_STATE = 64
NORM_EPS = 1e-6
GN_EPS = 1e-5

LANES = 128
S5_CHUNK = LANES
VMEM_BIG = 56 * 1024 * 1024
VMEM_MID = 40 * 1024 * 1024

FFN_TM = 512
FFN_FC = 1408
RET_TM = 512
ADA_TN = 1152
S5_TM = 1024


def _rms_mod(x, g, shift, scale):
    y = x * lax.rsqrt(jnp.mean(x * x, axis=-1, keepdims=True) + NORM_EPS)
    return (y * g) * (1.0 + scale) + shift


def _cmul(ar, ai, br, bi):
    return ar * br - ai * bi, ar * bi + ai * br


def _resident(shape):
    zeros = (0,) * len(shape)
    return pl.BlockSpec(shape, lambda *_: zeros, pipeline_mode=pl.Buffered(1))


def _ada_body(ct_ref, w_ref, b_ref, o_ref):
    ct = ct_ref[...]
    ca = ct * jax.nn.sigmoid(ct)
    w = w_ref[0]
    for b in range(ct.shape[1]):
        o_ref[0, b:b + 1, :] = jnp.sum(w * ca[:, b:b + 1], axis=0, keepdims=True) + b_ref[0]


def _ada(c, ada_w, ada_b):
    depth, d, n = ada_w.shape
    bsz = c.shape[0]
    return pl.pallas_call(
        _ada_body,
        grid=(depth, n // ADA_TN),
        in_specs=[
            pl.BlockSpec((d, bsz), lambda i, j: (0, 0)),
            pl.BlockSpec((1, d, ADA_TN), lambda i, j: (i, 0, j)),
            pl.BlockSpec((1, 1, ADA_TN), lambda i, j: (i, 0, j)),
        ],
        out_specs=pl.BlockSpec((1, bsz, ADA_TN), lambda i, j: (i, 0, j)),
        out_shape=jax.ShapeDtypeStruct((depth, bsz, n), F32),
        compiler_params=pltpu.CompilerParams(
            dimension_semantics=("arbitrary", "arbitrary"), vmem_limit_bytes=VMEM_MID),
        name="ada",
    )(c.T, ada_w, ada_b.reshape(depth, 1, n))


def _ffn_body(x_ref, ng_ref, sh_ref, sc_ref, gt_ref, win_ref, wout_ref, *rest, final):
    if final:
        fg_ref, o_ref = rest
    else:
        (o_ref,) = rest
    x = x_ref[...]
    h = _rms_mod(x, ng_ref[0], sh_ref[0], sc_ref[0]).astype(BF16)
    acc = None
    for j in range(D_FF // FFN_FC):
        lo = j * FFN_FC
        g = jnp.dot(h, win_ref[:, lo:lo + FFN_FC], preferred_element_type=F32)
        u = jnp.dot(h, win_ref[:, D_FF + lo:D_FF + lo + FFN_FC], preferred_element_type=F32)
        a = (g * jax.nn.sigmoid(g) * u).astype(BF16)
        p = jnp.dot(a, wout_ref[lo:lo + FFN_FC, :], preferred_element_type=F32)
        acc = p if acc is None else acc + p
    y = x + (0.5 * gt_ref[0]) * acc
    if final:
        y = (y * lax.rsqrt(jnp.mean(y * y, axis=-1, keepdims=True) + NORM_EPS)) * fg_ref[...]
    o_ref[...] = y


def _vec_spec(row_fn):
    return pl.BlockSpec((1, 1, D_MODEL), lambda *idx: (row_fn(*idx), 0, 0))


def _ffn(x2, ng3, mods3, norm_row, mod_row, w_in, w_out, tiles_per_batch, final_g=None):
    t, d = x2.shape
    final = final_g is not None
    in_specs = [
        pl.BlockSpec((FFN_TM, d), lambda i: (i, 0)),
        _vec_spec(lambda i: norm_row),
        _vec_spec(lambda i: mod_row(i // tiles_per_batch)),
        _vec_spec(lambda i: mod_row(i // tiles_per_batch) + 1),
        _vec_spec(lambda i: mod_row(i // tiles_per_batch) + 2),
        _resident(w_in.shape),
        _resident(w_out.shape),
    ]
    args = [x2, ng3, mods3, mods3, mods3, w_in, w_out]
    if final:
        in_specs.append(pl.BlockSpec((1, d), lambda i: (0, 0)))
        args.append(final_g.reshape(1, d))
    return pl.pallas_call(
        functools.partial(_ffn_body, final=final),
        grid=(t // FFN_TM,),
        in_specs=in_specs,
        out_specs=pl.BlockSpec((FFN_TM, d), lambda i: (i, 0)),
        out_shape=jax.ShapeDtypeStruct((t, d), F32),
        compiler_params=pltpu.CompilerParams(
            dimension_semantics=("arbitrary",), vmem_limit_bytes=VMEM_BIG),
        name="ffn_final" if final else "ffn",
    )(*args)


def _ret_in_body(x_ref, pos_ref, invf_ref, ng_ref, sh_ref, sc_ref, w_ref,
                 q_ref, k_ref, v_ref, g_ref):
    h = _rms_mod(x_ref[...], ng_ref[0], sh_ref[0], sc_ref[0]).astype(BF16)
    ang = pos_ref[...].astype(F32) * invf_ref[...]
    cos = jnp.cos(ang)
    sin = jnp.sin(ang)
    half = RET_QK_DIM // 2
    qk_w = RET_HEADS * RET_QK_DIM
    v_w = RET_HEADS * RET_V_DIM
    for base, dst, scale in ((0, q_ref, None), (qk_w, k_ref, RET_QK_DIM ** -0.5)):
        for hh in range(RET_HEADS):
            lo = hh * RET_QK_DIM
            t = jnp.dot(h, w_ref[:, base + lo:base + lo + RET_QK_DIM], preferred_element_type=F32)
            t1 = t[:, :half]
            t2 = t[:, half:]
            r1 = t1 * cos - t2 * sin
            r2 = t1 * sin + t2 * cos
            if scale is not None:
                r1 = r1 * scale
                r2 = r2 * scale
            dst[:, lo:lo + half] = r1.astype(BF16)
            dst[:, lo + half:lo + RET_QK_DIM] = r2.astype(BF16)
    for hh in range(RET_HEADS):
        lo = hh * RET_V_DIM
        v_ref[:, lo:lo + RET_V_DIM] = jnp.dot(
            h, w_ref[:, 2 * qk_w + lo:2 * qk_w + lo + RET_V_DIM],
            preferred_element_type=F32).astype(BF16)
        g_ref[:, lo:lo + RET_V_DIM] = jnp.dot(
            h, w_ref[:, 2 * qk_w + v_w + lo:2 * qk_w + v_w + lo + RET_V_DIM],
            preferred_element_type=F32).astype(BF16)


def _ret_in(x2, pos2, inv_freq, ng3, mods3, norm_row, mod_row, w_in, tiles_per_batch):
    t, d = x2.shape
    qk_w = RET_HEADS * RET_QK_DIM
    v_w = RET_HEADS * RET_V_DIM
    tok = lambda w: pl.BlockSpec((RET_TM, w), lambda i: (i, 0))
    return pl.pallas_call(
        _ret_in_body,
        grid=(t // RET_TM,),
        in_specs=[
            tok(d),
            tok(1),
            pl.BlockSpec((1, RET_QK_DIM // 2), lambda i: (0, 0)),
            _vec_spec(lambda i: norm_row),
            _vec_spec(lambda i: mod_row(i // tiles_per_batch)),
            _vec_spec(lambda i: mod_row(i // tiles_per_batch) + 1),
            _resident(w_in.shape),
        ],
        out_specs=[tok(qk_w), tok(qk_w), tok(v_w), tok(v_w)],
        out_shape=[jax.ShapeDtypeStruct((t, qk_w), BF16), jax.ShapeDtypeStruct((t, qk_w), BF16),
                   jax.ShapeDtypeStruct((t, v_w), BF16), jax.ShapeDtypeStruct((t, v_w), BF16)],
        compiler_params=pltpu.CompilerParams(
            dimension_semantics=("arbitrary",), vmem_limit_bytes=VMEM_BIG),
        name="ret_in",
    )(x2, pos2, inv_freq, ng3, mods3, mods3, w_in)


def _ret_core_body(q_ref, k_ref, v_ref, g_ref, x_ref, gt_ref, mask_ref, qd_ref, kd_ref, cd_ref,
                   wout_ref, o_ref, state_ref):
    @pl.when(pl.program_id(1) == 0)
    def _():
        state_ref[...] = jnp.zeros_like(state_ref)

    acc = None
    for hh in range(RET_HEADS):
        qh = q_ref[:, hh * RET_QK_DIM:(hh + 1) * RET_QK_DIM]
        kh = k_ref[:, hh * RET_QK_DIM:(hh + 1) * RET_QK_DIM]
        vh = v_ref[:, hh * RET_V_DIM:(hh + 1) * RET_V_DIM]
        gh = g_ref[:, hh * RET_V_DIM:(hh + 1) * RET_V_DIM].astype(F32)
        st = state_ref[hh]
        scores = lax.dot_general(qh, kh, (((1,), (1,)), ((), ())),
                                 preferred_element_type=F32) * mask_ref[hh]
        inner = jnp.dot(scores.astype(BF16), vh, preferred_element_type=F32)
        cross = jnp.dot(qh, st.astype(BF16), preferred_element_type=F32) * qd_ref[hh]
        o = inner + cross
        kdec = (kh.astype(F32) * kd_ref[hh]).astype(BF16)
        state_ref[hh] = st * cd_ref[hh] + lax.dot_general(
            kdec, vh, (((0,), (0,)), ((), ())), preferred_element_type=F32)
        mu = jnp.mean(o, axis=-1, keepdims=True)
        oc = o - mu
        var = jnp.mean(oc * oc, axis=-1, keepdims=True)
        on = oc * lax.rsqrt(var + GN_EPS)
        gated = ((gh * jax.nn.sigmoid(gh)) * on).astype(BF16)
        p = jnp.dot(gated, wout_ref[hh * RET_V_DIM:(hh + 1) * RET_V_DIM, :],
                    preferred_element_type=F32)
        acc = p if acc is None else acc + p
    o_ref[...] = x_ref[...] + gt_ref[0] * acc


def _ret_core(q, k, v, g, x2, mods3, gate_row, consts, w_out, bsz, n_chunks):
    t, d = x2.shape
    c = RET_CHUNK
    qk_w = RET_HEADS * RET_QK_DIM
    v_w = RET_HEADS * RET_V_DIM
    mask, qd, kd, cd = consts
    tok = lambda w: pl.BlockSpec((c, w), lambda b, n: (b * n_chunks + n, 0))
    full = lambda a: pl.BlockSpec(a.shape, lambda b, n: (0,) * a.ndim)
    return pl.pallas_call(
        _ret_core_body,
        grid=(bsz, n_chunks),
        in_specs=[tok(qk_w), tok(qk_w), tok(v_w), tok(v_w), tok(d),
                  _vec_spec(lambda b, n: gate_row(b)),
                  full(mask), full(qd), full(kd), full(cd),
                  _resident(w_out.shape)],
        out_specs=tok(d),
        out_shape=jax.ShapeDtypeStruct((t, d), F32),
        scratch_shapes=[pltpu.VMEM((RET_HEADS, RET_QK_DIM, RET_V_DIM), F32)],
        compiler_params=pltpu.CompilerParams(
            dimension_semantics=("arbitrary", "arbitrary"), vmem_limit_bytes=VMEM_MID),
        name="ret_core",
    )(q, k, v, g, x2, mods3, mask, qd, kd, cd, w_out)


def _ret_consts():
    hds, c = RET_HEADS, RET_CHUNK
    log_gamma = jnp.log(1.0 - jnp.power(2.0, -5.0 - jnp.arange(hds, dtype=F32)))
    pos = jnp.arange(c, dtype=F32)
    diff = pos[:, None] - pos[None, :]
    mask = jnp.where(diff >= 0, jnp.exp(log_gamma[:, None, None] * jnp.maximum(diff, 0.0)), 0.0)
    qd = jnp.exp(log_gamma[:, None] * (pos + 1.0))[:, :, None]
    kd = jnp.exp(log_gamma[:, None] * (c - 1.0 - pos))[:, :, None]
    cd = jnp.broadcast_to(jnp.exp(log_gamma * c)[:, None, None], (hds, 1, RET_V_DIM))
    return mask, qd, kd, cd


def _s5_pre_body(x_ref, ng_ref, sh_ref, sc_ref, o_ref):
    h = _rms_mod(x_ref[...], ng_ref[0], sh_ref[0], sc_ref[0])
    for r in range(S5_TM // S5_CHUNK):
        o_ref[r] = h[r * S5_CHUNK:(r + 1) * S5_CHUNK, :].T


def _s5_pre(x2, ng3, mods3, norm_row, mod_row, tiles_per_batch):
    t, d = x2.shape
    rows = S5_TM // S5_CHUNK
    return pl.pallas_call(
        _s5_pre_body,
        grid=(t // S5_TM,),
        in_specs=[pl.BlockSpec((S5_TM, d), lambda i: (i, 0)),
                  _vec_spec(lambda i: norm_row),
                  _vec_spec(lambda i: mod_row(i // tiles_per_batch)),
                  _vec_spec(lambda i: mod_row(i // tiles_per_batch) + 1)],
        out_specs=pl.BlockSpec((rows, d, S5_CHUNK), lambda i: (i, 0, 0)),
        out_shape=jax.ShapeDtypeStruct((t // S5_CHUNK, d, S5_CHUNK), F32),
        compiler_params=pltpu.CompilerParams(
            dimension_semantics=("arbitrary",), vmem_limit_bytes=VMEM_MID),
        name="s5_pre",
    )(x2, ng3, mods3, mods3)


def _s5_core_body(u_ref, are_ref, aim_ref, ldt_ref, btr_ref, bti_ref, cr_ref, ci_ref, d_ref,
                  o_ref, kt_ref, toep_ref, ucat_ref, ws_ref, wo_ref, *, chunks_per_seq):
    kk, pp, tt = S5_GROUP, S5_STATE, S5_CHUNK
    n_rows = u_ref.shape[0]
    a_re = are_ref[0]
    a_im = aim_ref[0]
    dt = jnp.exp(ldt_ref[0])
    lam_re = dt * a_re
    lam_im = dt * a_im
    mag = jnp.exp(lam_re)
    n_tab = tt.bit_length() - 1
    n_scan = (chunks_per_seq - 1).bit_length()
    pw = [(mag * jnp.cos(lam_im), mag * jnp.sin(lam_im))]
    for _ in range(n_tab + n_scan - 1):
        pw.append(_cmul(pw[-1][0], pw[-1][1], pw[-1][0], pw[-1][1]))

    def power_table(expo):
        er = jnp.ones((tt, pp), F32)
        ei = jnp.zeros((tt, pp), F32)
        for b in range(n_tab):
            nr, ni = _cmul(er, ei, pw[b][0], pw[b][1])
            bit = ((expo >> b) & 1) == 1
            er = jnp.where(bit, nr, er)
            ei = jnp.where(bit, ni, ei)
        return er, ei

    step = lax.broadcasted_iota(jnp.int32, (tt, pp), 0)
    e0r, e0i = power_table(step)
    e1r, e1i = _cmul(e0r, e0i, pw[0][0], pw[0][1])
    err, eri = power_table(tt - 1 - step)

    nr, ni = pw[0][0] - 1.0, pw[0][1]
    den = a_re * a_re + a_im * a_im
    q_re = (nr * a_re + ni * a_im) / den
    q_im = (ni * a_re - nr * a_im) / den
    bbr, bbi = _cmul(q_re, q_im, btr_ref[0], bti_ref[0])
    c_re = cr_ref[0]
    c_im = ci_ref[0]

    cb_r, cb_i = [], []
    for k in range(kk):
        r, i = _cmul(c_re[k:k + 1, :], c_im[k:k + 1, :], bbr, bbi)
        cb_r.append(r)
        cb_i.append(-i)
    cb = jnp.concatenate([jnp.concatenate(cb_r, axis=0), jnp.concatenate(cb_i, axis=0)], axis=1)
    e0 = jnp.concatenate([e0r, e0i], axis=1)
    kt_ref[...] = lax.dot_general(cb, e0, (((1,), (1,)), ((), ())),
                                  precision=lax.Precision.HIGHEST, preferred_element_type=F32)

    causal = (lax.broadcasted_iota(jnp.int32, (tt, tt), 1)
              >= lax.broadcasted_iota(jnp.int32, (tt, tt), 0))

    def toep_rows(kp, carry):
        row0 = pl.multiple_of(kp * tt, tt)
        for k in range(kk):
            lag = kt_ref[pl.ds(k * kk + kp, 1), :]
            blk = pltpu.roll(jnp.broadcast_to(lag, (tt, tt)), 0, 1, stride=1, stride_axis=0)
            toep_ref[pl.ds(row0, tt), k * tt:(k + 1) * tt] = jnp.where(causal, blk, 0.0).astype(BF16)
        return carry

    lax.fori_loop(0, kk, toep_rows, 0)

    for kp in range(kk):
        ucat_ref[:, kp * tt:(kp + 1) * tt] = u_ref[:, kp, :].astype(BF16)
        wr, wi = _cmul(err, eri, bbr[kp:kp + 1, :], bbi[kp:kp + 1, :])
        ws_ref[kp * tt:(kp + 1) * tt, :] = jnp.concatenate([wr, wi], axis=1).astype(BF16)
    ucat = ucat_ref[...]

    local = jnp.dot(ucat, ws_ref[...], preferred_element_type=F32)
    sr = local[:, :pp]
    si = local[:, pp:]
    cidx = lax.broadcasted_iota(jnp.int32, (n_rows, pp), 0) % chunks_per_seq
    for s in range(n_scan):
        sh = 1 << s
        tr, ti = _cmul(pltpu.roll(sr, sh, 0), pltpu.roll(si, sh, 0),
                       pw[n_tab + s][0], pw[n_tab + s][1])
        keep = cidx >= sh
        sr = sr + jnp.where(keep, tr, 0.0)
        si = si + jnp.where(keep, ti, 0.0)
    first = cidx >= 1
    prev = jnp.concatenate([jnp.where(first, pltpu.roll(sr, 1, 0), 0.0),
                            jnp.where(first, pltpu.roll(si, 1, 0), 0.0)], axis=1).astype(BF16)

    for k in range(kk):
        gr, gi = _cmul(e1r, e1i, c_re[k:k + 1, :], c_im[k:k + 1, :])
        wo_ref[k * tt:(k + 1) * tt, :] = jnp.concatenate([gr, -gi], axis=1).astype(BF16)

    y = jnp.dot(ucat, toep_ref[...], preferred_element_type=F32)
    y = y + lax.dot_general(prev, wo_ref[...], (((1,), (1,)), ((), ())),
                            preferred_element_type=F32)
    for k in range(kk):
        z = y[:, k * tt:(k + 1) * tt] + d_ref[0, k:k + 1, :] * u_ref[:, k, :]
        o_ref[:, k, :] = jax.nn.gelu(z)


def _s5_core(ht, a_re, a_im, log_dt, bt_re, bt_im, c_re, c_im, d_b, chunks_per_seq):
    rows, d, tt = ht.shape
    g, kk, pp = S5_GROUPS, S5_GROUP, S5_STATE
    grp = lambda *shape: pl.BlockSpec((1,) + shape, lambda i: (i,) + (0,) * len(shape))
    return pl.pallas_call(
        functools.partial(_s5_core_body, chunks_per_seq=chunks_per_seq),
        grid=(g,),
        in_specs=[pl.BlockSpec((rows, kk, tt), lambda i: (0, i, 0)),
                  grp(1, pp), grp(1, pp), grp(1, 1),
                  grp(kk, pp), grp(kk, pp), grp(kk, pp), grp(kk, pp),
                  grp(kk, tt)],
        out_specs=pl.BlockSpec((rows, kk, tt), lambda i: (0, i, 0)),
        out_shape=jax.ShapeDtypeStruct((rows, d, tt), F32),
        scratch_shapes=[pltpu.VMEM((kk * kk, tt), F32),
                        pltpu.VMEM((kk * tt, kk * tt), BF16),
                        pltpu.VMEM((rows, kk * tt), BF16),
                        pltpu.VMEM((kk * tt, 2 * pp), BF16),
                        pltpu.VMEM((kk * tt, 2 * pp), BF16)],
        compiler_params=pltpu.CompilerParams(
            dimension_semantics=("arbitrary",), vmem_limit_bytes=VMEM_MID),
        name="s5_core",
    )(ht, a_re, a_im, log_dt, bt_re, bt_im, c_re, c_im, d_b)


def _s5_post_body(y_ref, x_ref, gt_ref, w_ref, o_ref, ys_ref):
    for r in range(S5_TM // S5_CHUNK):
        ys_ref[r * S5_CHUNK:(r + 1) * S5_CHUNK, :] = y_ref[r].T.astype(BF16)
    yb = ys_ref[...]
    d = x_ref.shape[1]
    y1 = jnp.dot(yb, w_ref[:, :d], preferred_element_type=F32)
    y2 = jnp.dot(yb, w_ref[:, d:], preferred_element_type=F32)
    o_ref[...] = x_ref[...] + gt_ref[0] * (y1 * jax.nn.sigmoid(y2))


def _s5_post(yt, x2, mods3, gate_row, w_glu, tiles_per_batch):
    t, d = x2.shape
    rows = S5_TM // S5_CHUNK
    return pl.pallas_call(
        _s5_post_body,
        grid=(t // S5_TM,),
        in_specs=[pl.BlockSpec((rows, d, S5_CHUNK), lambda i: (i, 0, 0)),
                  pl.BlockSpec((S5_TM, d), lambda i: (i, 0)),
                  _vec_spec(lambda i: gate_row(i // tiles_per_batch)),
                  _resident(w_glu.shape)],
        out_specs=pl.BlockSpec((S5_TM, d), lambda i: (i, 0)),
        out_shape=jax.ShapeDtypeStruct((t, d), F32),
        scratch_shapes=[pltpu.VMEM((S5_TM, d), BF16)],
        compiler_params=pltpu.CompilerParams(
            dimension_semantics=("arbitrary",), vmem_limit_bytes=VMEM_BIG),
        name="s5_post",
    )(yt, x2, mods3, w_glu)


def kernel(x, c, positions, ada_w, ada_b, norm_g, ffn_w_in, ffn_w_out, ret_w_in, ret_w_out,
           s5_a_re, s5_a_im, s5_b_re, s5_b_im, s5_c_re, s5_c_im, s5_d, s5_log_dt, s5_w_glu,
           final_g):
    bsz, seq, d = x.shape
    depth = ada_w.shape[0]
    assert d == D_MODEL and seq % S5_TM == 0 and seq % RET_CHUNK == 0
    t = bsz * seq
    g, kk, pp = S5_GROUPS, S5_GROUP, S5_STATE

    mods3 = _ada(c, ada_w, ada_b).reshape(depth * bsz * N_MOD, 1, d)
    ng3 = norm_g.reshape(depth * 3, 1, d)
    x2 = x.reshape(t, d)
    pos2 = positions.reshape(t, 1)
    half = RET_QK_DIM // 2
    inv_freq = (ROPE_THETA ** (-jnp.arange(half, dtype=F32) / half)).reshape(1, half)
    ret_consts = _ret_consts()

    for i in range(depth):
        mod_row = lambda s: (lambda b: (i * bsz + b) * N_MOD + 3 * s)
        gate_row = lambda s: (lambda b: (i * bsz + b) * N_MOD + 3 * s + 2)
        x2 = _ffn(x2, ng3, mods3, i * 3, mod_row(0), ffn_w_in[i, 0].astype(BF16),
                  ffn_w_out[i, 0].astype(BF16), seq // FFN_TM)
        j = i // 2
        if i % 2 == 0:
            q, k, v, gq = _ret_in(x2, pos2, inv_freq, ng3, mods3, i * 3 + 1, mod_row(1),
                                  ret_w_in[j].astype(BF16), seq // RET_TM)
            x2 = _ret_core(q, k, v, gq, x2, mods3, gate_row(1), ret_consts,
                           ret_w_out[j].astype(BF16), bsz, seq // RET_CHUNK)
        else:
            ht = _s5_pre(x2, ng3, mods3, i * 3 + 1, mod_row(1), seq // S5_TM)
            yt = _s5_core(
                ht,
                s5_a_re[j].reshape(g, 1, pp), s5_a_im[j].reshape(g, 1, pp),
                s5_log_dt[j].reshape(g, 1, 1),
                s5_b_re[j].transpose(0, 2, 1), s5_b_im[j].transpose(0, 2, 1),
                s5_c_re[j], s5_c_im[j],
                jnp.broadcast_to(s5_d[j].reshape(g, kk, 1), (g, kk, S5_CHUNK)),
                seq // S5_CHUNK)
            x2 = _s5_post(yt, x2, mods3, gate_row(1), s5_w_glu[j].astype(BF16), seq // S5_TM)
        x2 = _ffn(x2, ng3, mods3, i * 3 + 2, mod_row(2), ffn_w_in[i, 1].astype(BF16),
                  ffn_w_out[i, 1].astype(BF16), seq // FFN_TM,
                  final_g=final_g if i == depth - 1 else None)
    return x2.reshape(bsz, seq, d)
```

```python
import functools

import jax
import jax.numpy as jnp
from jax import lax
from jax.experimental import pallas as pl
from jax.experimental.pallas import tpu as pltpu

F32 = jnp.float32
BF16 = jnp.bfloat16

D_MODEL = 1024
N_MOD = 9
D_FF = 2816
RET_HEADS = 4
RET_QK_DIM = 256
RET_V_DIM = 512
RET_CHUNK = 128
ROPE_THETA = 10000.0
S5_GROUP = 16
S5_GROUPS = 64
S5_STATE = 64
NORM_EPS = 1e-6
GN_EPS = 1e-5

LANES = 128
SUBLANES = 8
MXU_DIM = 256
S5_CHUNK = LANES
VMEM_BIG = 56 * 1024 * 1024
VMEM_MID = 40 * 1024 * 1024

FFN_TM = 512
FFN_CHUNKS = ((0, 6 * MXU_DIM), (6 * MXU_DIM, D_FF))
assert D_FF % MXU_DIM == 0
RET_TM = 512
ADA_TN = 1152
S5_TM = 1024


def _rms_mod(x, g, shift, scale):
    y = x * lax.rsqrt(jnp.mean(x * x, axis=-1, keepdims=True) + NORM_EPS)
    return (y * g) * (1.0 + scale) + shift


def _cmul(ar, ai, br, bi):
    return ar * br - ai * bi, ar * bi + ai * br


def _resident(shape):
    zeros = (0,) * len(shape)
    return pl.BlockSpec(shape, lambda *_: zeros, pipeline_mode=pl.Buffered(1))


def _ada_body(ct_ref, w_ref, b_ref, o_ref):
    ct = ct_ref[...]
    ca = ct * jax.nn.sigmoid(ct)
    w = w_ref[0]
    for b in range(ct.shape[1]):
        o_ref[0, b:b + 1, :] = jnp.sum(w * ca[:, b:b + 1], axis=0, keepdims=True) + b_ref[0]


def _ada(c, ada_w, ada_b):
    depth, d, n = ada_w.shape
    bsz = c.shape[0]
    return pl.pallas_call(
        _ada_body,
        grid=(depth, n // ADA_TN),
        in_specs=[
            pl.BlockSpec((d, bsz), lambda i, j: (0, 0)),
            pl.BlockSpec((1, d, ADA_TN), lambda i, j: (i, 0, j)),
            pl.BlockSpec((1, 1, ADA_TN), lambda i, j: (i, 0, j)),
        ],
        out_specs=pl.BlockSpec((1, bsz, ADA_TN), lambda i, j: (i, 0, j)),
        out_shape=jax.ShapeDtypeStruct((depth, bsz, n), F32),
        compiler_params=pltpu.CompilerParams(
            dimension_semantics=("arbitrary", "arbitrary"), vmem_limit_bytes=VMEM_MID),
        name="ada",
    )(c.T, ada_w, ada_b.reshape(depth, 1, n))


def _ffn_body(x_ref, ng_ref, sh_ref, sc_ref, gt_ref, win_ref, wout_ref, *rest, final):
    if final:
        fg_ref, o_ref = rest
    else:
        (o_ref,) = rest
    x = x_ref[...]
    h = _rms_mod(x, ng_ref[0], sh_ref[0], sc_ref[0]).astype(BF16)
    acc = None
    for lo, hi in FFN_CHUNKS:
        g = jnp.dot(h, win_ref[:, lo:hi], preferred_element_type=F32)
        u = jnp.dot(h, win_ref[:, D_FF + lo:D_FF + hi], preferred_element_type=F32)
        a = (g * jax.nn.sigmoid(g) * u).astype(BF16)
        p = jnp.dot(a, wout_ref[lo:hi, :], preferred_element_type=F32)
        acc = p if acc is None else acc + p
    y = x + (0.5 * gt_ref[0]) * acc
    if final:
        y = (y * lax.rsqrt(jnp.mean(y * y, axis=-1, keepdims=True) + NORM_EPS)) * fg_ref[...]
    o_ref[...] = y


def _vec_spec(row_fn):
    return pl.BlockSpec((1, 1, D_MODEL), lambda *idx: (row_fn(*idx), 0, 0))


def _ffn(x2, ng3, mods3, norm_row, mod_row, w_in, w_out, tiles_per_batch, final_g=None):
    t, d = x2.shape
    final = final_g is not None
    in_specs = [
        pl.BlockSpec((FFN_TM, d), lambda i: (i, 0)),
        _vec_spec(lambda i: norm_row),
        _vec_spec(lambda i: mod_row(i // tiles_per_batch)),
        _vec_spec(lambda i: mod_row(i // tiles_per_batch) + 1),
        _vec_spec(lambda i: mod_row(i // tiles_per_batch) + 2),
        _resident(w_in.shape),
        _resident(w_out.shape),
    ]
    args = [x2, ng3, mods3, mods3, mods3, w_in, w_out]
    if final:
        in_specs.append(pl.BlockSpec((1, d), lambda i: (0, 0)))
        args.append(final_g.reshape(1, d))
    return pl.pallas_call(
        functools.partial(_ffn_body, final=final),
        grid=(t // FFN_TM,),
        in_specs=in_specs,
        out_specs=pl.BlockSpec((FFN_TM, d), lambda i: (i, 0)),
        out_shape=jax.ShapeDtypeStruct((t, d), F32),
        compiler_params=pltpu.CompilerParams(
            dimension_semantics=("arbitrary",), vmem_limit_bytes=VMEM_BIG),
        name="ffn_final" if final else "ffn",
    )(*args)


def _ret_in_body(x_ref, pos_ref, invf_ref, ng_ref, sh_ref, sc_ref, w_ref,
                 q_ref, k_ref, v_ref, g_ref):
    h = _rms_mod(x_ref[...], ng_ref[0], sh_ref[0], sc_ref[0]).astype(BF16)
    half = RET_QK_DIM // 2
    qk_w = RET_HEADS * RET_QK_DIM
    v_w = RET_HEADS * RET_V_DIM
    for hh in range(RET_HEADS):
        lo = hh * RET_V_DIM
        v_ref[:, lo:lo + RET_V_DIM] = jnp.dot(
            h, w_ref[:, 2 * qk_w + lo:2 * qk_w + lo + RET_V_DIM],
            preferred_element_type=F32).astype(BF16)
        g_ref[:, lo:lo + RET_V_DIM] = jnp.dot(
            h, w_ref[:, 2 * qk_w + v_w + lo:2 * qk_w + v_w + lo + RET_V_DIM],
            preferred_element_type=F32).astype(BF16)
    ang = pos_ref[...].astype(F32) * invf_ref[...]
    cos = jnp.cos(ang)
    sin = jnp.sin(ang)
    for base, dst, scale in ((0, q_ref, None), (qk_w, k_ref, RET_QK_DIM ** -0.5)):
        for hh in range(RET_HEADS):
            lo = hh * RET_QK_DIM
            t = jnp.dot(h, w_ref[:, base + lo:base + lo + RET_QK_DIM], preferred_element_type=F32)
            t1 = t[:, :half]
            t2 = t[:, half:]
            r1 = t1 * cos - t2 * sin
            r2 = t1 * sin + t2 * cos
            if scale is not None:
                r1 = r1 * scale
                r2 = r2 * scale
            dst[:, lo:lo + half] = r1.astype(BF16)
            dst[:, lo + half:lo + RET_QK_DIM] = r2.astype(BF16)


def _ret_in(x2, pos2, inv_freq, ng3, mods3, norm_row, mod_row, w_in, tiles_per_batch):
    t, d = x2.shape
    qk_w = RET_HEADS * RET_QK_DIM
    v_w = RET_HEADS * RET_V_DIM
    tok = lambda w: pl.BlockSpec((RET_TM, w), lambda i: (i, 0))
    return pl.pallas_call(
        _ret_in_body,
        grid=(t // RET_TM,),
        in_specs=[
            tok(d),
            tok(1),
            pl.BlockSpec((1, RET_QK_DIM // 2), lambda i: (0, 0)),
            _vec_spec(lambda i: norm_row),
            _vec_spec(lambda i: mod_row(i // tiles_per_batch)),
            _vec_spec(lambda i: mod_row(i // tiles_per_batch) + 1),
            _resident(w_in.shape),
        ],
        out_specs=[tok(qk_w), tok(qk_w), tok(v_w), tok(v_w)],
        out_shape=[jax.ShapeDtypeStruct((t, qk_w), BF16), jax.ShapeDtypeStruct((t, qk_w), BF16),
                   jax.ShapeDtypeStruct((t, v_w), BF16), jax.ShapeDtypeStruct((t, v_w), BF16)],
        compiler_params=pltpu.CompilerParams(
            dimension_semantics=("arbitrary",), vmem_limit_bytes=VMEM_BIG),
        name="ret_in",
    )(x2, pos2, inv_freq, ng3, mods3, mods3, w_in)


def _ret_core_body(q_ref, k_ref, v_ref, g_ref, x_ref, gt_ref, mask_ref, qd_ref, kd_ref, cd_ref,
                   wout_ref, o_ref, state_ref):
    @pl.when(pl.program_id(1) == 0)
    def _():
        state_ref[...] = jnp.zeros_like(state_ref)

    acc = None
    for hh in range(RET_HEADS):
        qh = q_ref[:, hh * RET_QK_DIM:(hh + 1) * RET_QK_DIM]
        kh = k_ref[:, hh * RET_QK_DIM:(hh + 1) * RET_QK_DIM]
        vh = v_ref[:, hh * RET_V_DIM:(hh + 1) * RET_V_DIM]
        gh = g_ref[:, hh * RET_V_DIM:(hh + 1) * RET_V_DIM].astype(F32)
        st = state_ref[hh]
        scores = lax.dot_general(qh, kh, (((1,), (1,)), ((), ())),
                                 preferred_element_type=F32) * mask_ref[hh]
        inner = jnp.dot(scores.astype(BF16), vh, preferred_element_type=F32)
        cross = jnp.dot(qh, st.astype(BF16), preferred_element_type=F32) * qd_ref[hh]
        o = inner + cross
        kdec = (kh.astype(F32) * kd_ref[hh]).astype(BF16)
        state_ref[hh] = st * cd_ref[hh] + lax.dot_general(
            kdec, vh, (((0,), (0,)), ((), ())), preferred_element_type=F32)
        mu = jnp.mean(o, axis=-1, keepdims=True)
        oc = o - mu
        var = jnp.mean(oc * oc, axis=-1, keepdims=True)
        on = oc * lax.rsqrt(var + GN_EPS)
        gated = ((gh * jax.nn.sigmoid(gh)) * on).astype(BF16)
        p = jnp.dot(gated, wout_ref[hh * RET_V_DIM:(hh + 1) * RET_V_DIM, :],
                    preferred_element_type=F32)
        acc = p if acc is None else acc + p
    o_ref[...] = x_ref[...] + gt_ref[0] * acc


def _ret_core(q, k, v, g, x2, mods3, gate_row, consts, w_out, bsz, n_chunks):
    t, d = x2.shape
    c = RET_CHUNK
    qk_w = RET_HEADS * RET_QK_DIM
    v_w = RET_HEADS * RET_V_DIM
    mask, qd, kd, cd = consts
    tok = lambda w: pl.BlockSpec((c, w), lambda b, n: (b * n_chunks + n, 0))
    full = lambda a: pl.BlockSpec(a.shape, lambda b, n: (0,) * a.ndim)
    return pl.pallas_call(
        _ret_core_body,
        grid=(bsz, n_chunks),
        in_specs=[tok(qk_w), tok(qk_w), tok(v_w), tok(v_w), tok(d),
                  _vec_spec(lambda b, n: gate_row(b)),
                  full(mask), full(qd), full(kd), full(cd),
                  _resident(w_out.shape)],
        out_specs=tok(d),
        out_shape=jax.ShapeDtypeStruct((t, d), F32),
        scratch_shapes=[pltpu.VMEM((RET_HEADS, RET_QK_DIM, RET_V_DIM), F32)],
        compiler_params=pltpu.CompilerParams(
            dimension_semantics=("arbitrary", "arbitrary"), vmem_limit_bytes=VMEM_MID),
        name="ret_core",
    )(q, k, v, g, x2, mods3, mask, qd, kd, cd, w_out)


def _ret_consts():
    hds, c = RET_HEADS, RET_CHUNK
    log_gamma = jnp.log(1.0 - jnp.power(2.0, -5.0 - jnp.arange(hds, dtype=F32)))
    pos = jnp.arange(c, dtype=F32)
    diff = pos[:, None] - pos[None, :]
    mask = jnp.where(diff >= 0, jnp.exp(log_gamma[:, None, None] * jnp.maximum(diff, 0.0)), 0.0)
    qd = jnp.exp(log_gamma[:, None] * (pos + 1.0))[:, :, None]
    kd = jnp.exp(log_gamma[:, None] * (c - 1.0 - pos))[:, :, None]
    cd = jnp.broadcast_to(jnp.exp(log_gamma * c)[:, None, None], (hds, 1, RET_V_DIM))
    return mask, qd, kd, cd


def _s5_pre_body(x_ref, ng_ref, sh_ref, sc_ref, o_ref):
    h = _rms_mod(x_ref[...], ng_ref[0], sh_ref[0], sc_ref[0])
    d = h.shape[1]
    for r in range(S5_TM // S5_CHUNK):
        ht = h[r * S5_CHUNK:(r + 1) * S5_CHUNK, :].T
        o_ref[:, r * SUBLANES:(r + 1) * SUBLANES, :] = ht.reshape(d // SUBLANES, SUBLANES, S5_CHUNK)


def _s5_pre(x2, ng3, mods3, norm_row, mod_row, tiles_per_batch):
    t, d = x2.shape
    rows = S5_TM // S5_CHUNK * SUBLANES
    return pl.pallas_call(
        _s5_pre_body,
        grid=(t // S5_TM,),
        in_specs=[pl.BlockSpec((S5_TM, d), lambda i: (i, 0)),
                  _vec_spec(lambda i: norm_row),
                  _vec_spec(lambda i: mod_row(i // tiles_per_batch)),
                  _vec_spec(lambda i: mod_row(i // tiles_per_batch) + 1)],
        out_specs=pl.BlockSpec((d // SUBLANES, rows, S5_CHUNK), lambda i: (0, i, 0)),
        out_shape=jax.ShapeDtypeStruct((d // SUBLANES, t // S5_CHUNK * SUBLANES, S5_CHUNK), F32),
        compiler_params=pltpu.CompilerParams(
            dimension_semantics=("arbitrary",), vmem_limit_bytes=VMEM_MID),
        name="s5_pre",
    )(x2, ng3, mods3, mods3)


def _pair(re, im):
    return jnp.concatenate([re, im], axis=1)


def _cmul2(x, xs, w1, w2):
    return x * w1 + xs * w2, xs * w1 - x * w2


def _s5_core_body(*refs, chunks_per_seq):
    kk, pp, tt = S5_GROUP, S5_STATE, S5_CHUNK
    n_half = kk // SUBLANES
    u_refs = refs[:n_half]
    are_ref, aim_ref, ldt_ref, btr_ref, bti_ref, cr_ref, ci_ref, d_ref = refs[n_half:n_half + 8]
    o_refs = refs[n_half + 8:2 * n_half + 8]
    kt_ref, toep_ref, ucat_ref, ws_ref, wo_ref = refs[2 * n_half + 8:]
    n_rows = u_refs[0].shape[0] // SUBLANES

    def chan(block_refs, k):
        return block_refs[k // SUBLANES], pl.ds(k % SUBLANES, n_rows, stride=SUBLANES)

    a_re = are_ref[0]
    a_im = aim_ref[0]
    dt = jnp.exp(ldt_ref[0])
    lam_re = dt * a_re
    lam_im = dt * a_im
    mag = jnp.exp(lam_re)
    n_tab = tt.bit_length() - 1
    n_scan = (chunks_per_seq - 1).bit_length()
    n_lvl = n_tab + n_scan
    sq = [(mag * jnp.cos(lam_im), mag * jnp.sin(lam_im))]
    for _ in range(n_lvl - 1):
        sq.append(_cmul(sq[-1][0], sq[-1][1], sq[-1][0], sq[-1][1]))
    lvl_re = jnp.concatenate([r for r, _ in sq], axis=0)
    lvl_im = jnp.concatenate([i for _, i in sq], axis=0)
    w1_all = _pair(lvl_re, lvl_re)
    w2_all = _pair(-lvl_im, lvl_im)
    w1 = lambda b: w1_all[b:b + 1, :]
    w2 = lambda b: w2_all[b:b + 1, :]

    def power_table(first, ascending):
        lane = lax.broadcasted_iota(jnp.int32, (SUBLANES, 2 * pp), 1)
        x = jnp.where(lane < pp, 1.0, 0.0).astype(F32)
        xs = 1.0 - x
        for b in range(SUBLANES.bit_length()):
            nx, nxs = _cmul2(x, xs, w1(b), w2(b))
            bit = ((first >> b) & 1) == 1
            x = jnp.where(bit, nx, x)
            xs = jnp.where(bit, nxs, xs)
        for b in range(SUBLANES.bit_length() - 1, n_tab):
            nx, nxs = _cmul2(x, xs, w1(b), w2(b))
            x = jnp.concatenate([x, nx] if ascending else [nx, x], axis=0)
            xs = jnp.concatenate([xs, nxs] if ascending else [nxs, xs], axis=0)
        return x, xs

    sub = lax.broadcasted_iota(jnp.int32, (SUBLANES, 2 * pp), 0)
    e0, _ = power_table(sub, True)
    e1, e1s = power_table(sub + 1, True)
    er, ers = power_table(SUBLANES - 1 - sub, False)

    nr, ni = sq[0][0] - 1.0, sq[0][1]
    den = a_re * a_re + a_im * a_im
    q_re = (nr * a_re + ni * a_im) / den
    q_im = (ni * a_re - nr * a_im) / den
    bbr, bbi = _cmul(q_re, q_im, btr_ref[0], bti_ref[0])
    bb = _pair(bbr, bbi)
    bbs = _pair(bbi, bbr)
    b1 = _pair(bbr, bbr)
    b2 = _pair(-bbi, bbi)
    c_re = cr_ref[0]
    c_im = ci_ref[0]
    c1 = _pair(c_re, -c_re)
    c2 = _pair(-c_im, -c_im)

    cb = jnp.concatenate([bb * c1[k:k + 1, :] + bbs * c2[k:k + 1, :] for k in range(kk)], axis=0)
    kt_ref[...] = lax.dot_general(cb, e0, (((1,), (1,)), ((), ())),
                                  precision=lax.Precision.HIGHEST, preferred_element_type=F32)

    causal = (lax.broadcasted_iota(jnp.int32, (tt, tt), 1)
              >= lax.broadcasted_iota(jnp.int32, (tt, tt), 0))

    def toep_rows(kp, carry):
        row0 = pl.multiple_of(kp * tt, tt)
        for k in range(kk):
            lag = kt_ref[pl.ds(k * kk + kp, 1), :]
            blk = pltpu.roll(jnp.broadcast_to(lag, (tt, tt)), 0, 1, stride=1, stride_axis=0)
            toep_ref[pl.ds(row0, tt), k * tt:(k + 1) * tt] = jnp.where(causal, blk, 0.0).astype(BF16)
        return carry

    lax.fori_loop(0, kk, toep_rows, 0)

    us = []
    for k in range(kk):
        flat, rows_k = chan(u_refs, k)
        us.append(flat[rows_k, :])
    for kp in range(kk):
        ucat_ref[:, kp * tt:(kp + 1) * tt] = us[kp].astype(BF16)
        ws_ref[kp * tt:(kp + 1) * tt, :] = (er * b1[kp:kp + 1, :] + ers * b2[kp:kp + 1, :]).astype(BF16)
    ucat = ucat_ref[...]

    s = jnp.dot(ucat, ws_ref[...], preferred_element_type=F32)
    ss = pltpu.roll(s, pp, 1)
    cidx = lax.broadcasted_iota(jnp.int32, (n_rows, 2 * pp), 0) % chunks_per_seq
    for lvl in range(n_scan):
        sh = 1 << lvl
        tx, txs = _cmul2(pltpu.roll(s, sh, 0), pltpu.roll(ss, sh, 0), w1(n_tab + lvl), w2(n_tab + lvl))
        keep = cidx >= sh
        s = s + jnp.where(keep, tx, 0.0)
        ss = ss + jnp.where(keep, txs, 0.0)
    prev = jnp.where(cidx >= 1, pltpu.roll(s, 1, 0), 0.0).astype(BF16)

    for k in range(kk):
        wo_ref[k * tt:(k + 1) * tt, :] = (e1 * c1[k:k + 1, :] + e1s * c2[k:k + 1, :]).astype(BF16)

    y = jnp.dot(ucat, toep_ref[...], preferred_element_type=F32)
    y = y + lax.dot_general(prev, wo_ref[...], (((1,), (1,)), ((), ())),
                            preferred_element_type=F32)
    for k in range(kk):
        z = y[:, k * tt:(k + 1) * tt] + d_ref[0, k:k + 1, :] * us[k]
        flat, rows_k = chan(o_refs, k)
        flat[rows_k, :] = jax.nn.gelu(z)


def _s5_core(ht, a_re, a_im, log_dt, bt_re, bt_im, c_re, c_im, d_b, chunks_per_seq):
    _, flat_rows, tt = ht.shape
    rows = flat_rows // SUBLANES
    g, kk, pp = S5_GROUPS, S5_GROUP, S5_STATE
    grp = lambda *shape: pl.BlockSpec((1,) + shape, lambda i: (i,) + (0,) * len(shape))
    n_half = kk // SUBLANES
    halves = [pl.BlockSpec((None, flat_rows, tt), lambda i, h=h: (i * n_half + h, 0, 0))
              for h in range(n_half)]
    return pl.pallas_call(
        functools.partial(_s5_core_body, chunks_per_seq=chunks_per_seq),
        grid=(g,),
        in_specs=halves + [grp(1, pp), grp(1, pp), grp(1, 1),
                           grp(kk, pp), grp(kk, pp), grp(kk, pp), grp(kk, pp),
                           grp(kk, tt)],
        out_specs=[pl.BlockSpec((None, flat_rows, tt), lambda i: (i, 0, 0))] * n_half,
        out_shape=[jax.ShapeDtypeStruct((g, flat_rows, tt), F32)] * n_half,
        scratch_shapes=[pltpu.VMEM((kk * kk, tt), F32),
                        pltpu.VMEM((kk * tt, kk * tt), BF16),
                        pltpu.VMEM((rows, kk * tt), BF16),
                        pltpu.VMEM((kk * tt, 2 * pp), BF16),
                        pltpu.VMEM((kk * tt, 2 * pp), BF16)],
        compiler_params=pltpu.CompilerParams(
            dimension_semantics=("arbitrary",), vmem_limit_bytes=VMEM_MID),
        name="s5_core",
    )(*([ht] * n_half), a_re, a_im, log_dt, bt_re, bt_im, c_re, c_im, d_b)


def _s5_post_body(*refs):
    n_half = S5_GROUP // SUBLANES
    y_refs = refs[:n_half]
    x_ref, gt_ref, w_ref, o_ref, ys_ref = refs[n_half:]
    for r in range(S5_TM // S5_CHUNK):
        yt = jnp.concatenate(
            [y_refs[h][:, r * SUBLANES:(r + 1) * SUBLANES, :].reshape(S5_GROUPS * SUBLANES, S5_CHUNK)
             for h in range(n_half)], axis=0)
        ys_ref[r * S5_CHUNK:(r + 1) * S5_CHUNK, :] = yt.T.astype(BF16)
    yb = ys_ref[...]
    d = x_ref.shape[1]
    y1 = jnp.dot(yb, w_ref[:, :d], preferred_element_type=F32)
    y2 = jnp.dot(yb, w_ref[:, d:], preferred_element_type=F32)
    o_ref[...] = x_ref[...] + gt_ref[0] * (y1 * jax.nn.sigmoid(y2))


def _s5_post(yts, x2, mods3, gate_row, w_glu, tiles_per_batch):
    t, d = x2.shape
    rows = S5_TM // S5_CHUNK
    return pl.pallas_call(
        _s5_post_body,
        grid=(t // S5_TM,),
        in_specs=[pl.BlockSpec((S5_GROUPS, rows * SUBLANES, S5_CHUNK),
                               lambda i: (0, i, 0))] * (S5_GROUP // SUBLANES) + [
                  pl.BlockSpec((S5_TM, d), lambda i: (i, 0)),
                  _vec_spec(lambda i: gate_row(i // tiles_per_batch)),
                  _resident(w_glu.shape)],
        out_specs=pl.BlockSpec((S5_TM, d), lambda i: (i, 0)),
        out_shape=jax.ShapeDtypeStruct((t, d), F32),
        scratch_shapes=[pltpu.VMEM((S5_TM, d), BF16)],
        compiler_params=pltpu.CompilerParams(
            dimension_semantics=("arbitrary",), vmem_limit_bytes=VMEM_BIG),
        name="s5_post",
    )(*yts, x2, mods3, w_glu)


def kernel(x, c, positions, ada_w, ada_b, norm_g, ffn_w_in, ffn_w_out, ret_w_in, ret_w_out,
           s5_a_re, s5_a_im, s5_b_re, s5_b_im, s5_c_re, s5_c_im, s5_d, s5_log_dt, s5_w_glu,
           final_g):
    bsz, seq, d = x.shape
    depth = ada_w.shape[0]
    assert d == D_MODEL and seq % S5_TM == 0 and seq % RET_CHUNK == 0
    t = bsz * seq
    g, kk, pp = S5_GROUPS, S5_GROUP, S5_STATE

    mods3 = _ada(c, ada_w, ada_b).reshape(depth * bsz * N_MOD, 1, d)
    ng3 = norm_g.reshape(depth * 3, 1, d)
    x2 = x.reshape(t, d)
    pos2 = positions.reshape(t, 1)
    half = RET_QK_DIM // 2
    inv_freq = (ROPE_THETA ** (-jnp.arange(half, dtype=F32) / half)).reshape(1, half)
    ret_consts = _ret_consts()

    for i in range(depth):
        mod_row = lambda s: (lambda b: (i * bsz + b) * N_MOD + 3 * s)
        gate_row = lambda s: (lambda b: (i * bsz + b) * N_MOD + 3 * s + 2)
        x2 = _ffn(x2, ng3, mods3, i * 3, mod_row(0), ffn_w_in[i, 0].astype(BF16),
                  ffn_w_out[i, 0].astype(BF16), seq // FFN_TM)
        j = i // 2
        if i % 2 == 0:
            q, k, v, gq = _ret_in(x2, pos2, inv_freq, ng3, mods3, i * 3 + 1, mod_row(1),
                                  ret_w_in[j].astype(BF16), seq // RET_TM)
            x2 = _ret_core(q, k, v, gq, x2, mods3, gate_row(1), ret_consts,
                           ret_w_out[j].astype(BF16), bsz, seq // RET_CHUNK)
        else:
            ht = _s5_pre(x2, ng3, mods3, i * 3 + 1, mod_row(1), seq // S5_TM)
            yts = _s5_core(
                ht,
                s5_a_re[j].reshape(g, 1, pp), s5_a_im[j].reshape(g, 1, pp),
                s5_log_dt[j].reshape(g, 1, 1),
                s5_b_re[j].transpose(0, 2, 1), s5_b_im[j].transpose(0, 2, 1),
                s5_c_re[j], s5_c_im[j],
                jnp.broadcast_to(s5_d[j].reshape(g, kk, 1), (g, kk, S5_CHUNK)),
                seq // S5_CHUNK)
            w_glu = s5_w_glu[j].reshape(g, kk // SUBLANES, SUBLANES, -1).transpose(1, 0, 2, 3)
            w_glu = w_glu.reshape(d, -1).astype(BF16)
            x2 = _s5_post(yts, x2, mods3, gate_row(1), w_glu, seq // S5_TM)
        x2 = _ffn(x2, ng3, mods3, i * 3 + 2, mod_row(2), ffn_w_in[i, 1].astype(BF16),
                  ffn_w_out[i, 1].astype(BF16), seq // FFN_TM,
                  final_g=final_g if i == depth - 1 else None)
    return x2.reshape(bsz, seq, d)
```

```python
import functools

import jax
import jax.numpy as jnp
from jax import lax
from jax.experimental import pallas as pl
from jax.experimental.pallas import tpu as pltpu

F32 = jnp.float32
BF16 = jnp.bfloat16

D_MODEL = 1024
N_MOD = 9
D_FF = 2816
RET_HEADS = 4
RET_QK_DIM = 256
RET_V_DIM = 512
RET_CHUNK = 128
ROPE_THETA = 10000.0
S5_GROUP = 16
S5_GROUPS = 64
S5_STATE = 64
NORM_EPS = 1e-6
GN_EPS = 1e-5

LANES = 128
SUBLANES = 8
MXU_DIM = 256
S5_CHUNK = LANES
VMEM_BIG = 56 * 1024 * 1024
VMEM_MID = 40 * 1024 * 1024

FFN_TM = 512
FFN_CHUNKS = ((0, 6 * MXU_DIM), (6 * MXU_DIM, D_FF))
assert D_FF % MXU_DIM == 0
RET_TM = 512
ADA_TN = 1152
S5_TM = 1024


def _rms_mod(x, g, shift, scale):
    y = x * lax.rsqrt(jnp.mean(x * x, axis=-1, keepdims=True) + NORM_EPS)
    return (y * g) * (1.0 + scale) + shift


def _cmul(ar, ai, br, bi):
    return ar * br - ai * bi, ar * bi + ai * br


def _resident(shape):
    zeros = (0,) * len(shape)
    return pl.BlockSpec(shape, lambda *_: zeros, pipeline_mode=pl.Buffered(1))


def _ada_body(ct_ref, w_ref, b_ref, o_ref):
    ct = ct_ref[...]
    ca = ct * jax.nn.sigmoid(ct)
    w = w_ref[0]
    for b in range(ct.shape[1]):
        o_ref[0, b:b + 1, :] = jnp.sum(w * ca[:, b:b + 1], axis=0, keepdims=True) + b_ref[0]


def _ada(c, ada_w, ada_b):
    depth, d, n = ada_w.shape
    bsz = c.shape[0]
    return pl.pallas_call(
        _ada_body,
        grid=(depth, n // ADA_TN),
        in_specs=[
            pl.BlockSpec((d, bsz), lambda i, j: (0, 0)),
            pl.BlockSpec((1, d, ADA_TN), lambda i, j: (i, 0, j)),
            pl.BlockSpec((1, 1, ADA_TN), lambda i, j: (i, 0, j)),
        ],
        out_specs=pl.BlockSpec((1, bsz, ADA_TN), lambda i, j: (i, 0, j)),
        out_shape=jax.ShapeDtypeStruct((depth, bsz, n), F32),
        compiler_params=pltpu.CompilerParams(
            dimension_semantics=("arbitrary", "arbitrary"), vmem_limit_bytes=VMEM_MID),
        name="ada",
    )(c.T, ada_w, ada_b.reshape(depth, 1, n))


def _ffn_body(x_ref, ng_ref, sh_ref, sc_ref, gt_ref, win_ref, wout_ref, *rest, final):
    if final:
        fg_ref, o_ref = rest
    else:
        (o_ref,) = rest
    x = x_ref[...]
    h = _rms_mod(x, ng_ref[0], sh_ref[0], sc_ref[0]).astype(BF16)
    acc = None
    for lo, hi in FFN_CHUNKS:
        g = jnp.dot(h, win_ref[:, lo:hi], preferred_element_type=F32)
        u = jnp.dot(h, win_ref[:, D_FF + lo:D_FF + hi], preferred_element_type=F32)
        a = (g * jax.nn.sigmoid(g) * u).astype(BF16)
        p = jnp.dot(a, wout_ref[lo:hi, :], preferred_element_type=F32)
        acc = p if acc is None else acc + p
    y = x + (0.5 * gt_ref[0]) * acc
    if final:
        y = (y * lax.rsqrt(jnp.mean(y * y, axis=-1, keepdims=True) + NORM_EPS)) * fg_ref[...]
    o_ref[...] = y


def _vec_spec(row_fn):
    return pl.BlockSpec((1, 1, D_MODEL), lambda *idx: (row_fn(*idx), 0, 0))


def _ffn(x2, ng3, mods3, norm_row, mod_row, w_in, w_out, tiles_per_batch, final_g=None):
    t, d = x2.shape
    final = final_g is not None
    in_specs = [
        pl.BlockSpec((FFN_TM, d), lambda i: (i, 0)),
        _vec_spec(lambda i: norm_row),
        _vec_spec(lambda i: mod_row(i // tiles_per_batch)),
        _vec_spec(lambda i: mod_row(i // tiles_per_batch) + 1),
        _vec_spec(lambda i: mod_row(i // tiles_per_batch) + 2),
        _resident(w_in.shape),
        _resident(w_out.shape),
    ]
    args = [x2, ng3, mods3, mods3, mods3, w_in, w_out]
    if final:
        in_specs.append(pl.BlockSpec((1, d), lambda i: (0, 0)))
        args.append(final_g.reshape(1, d))
    return pl.pallas_call(
        functools.partial(_ffn_body, final=final),
        grid=(t // FFN_TM,),
        in_specs=in_specs,
        out_specs=pl.BlockSpec((FFN_TM, d), lambda i: (i, 0)),
        out_shape=jax.ShapeDtypeStruct((t, d), F32),
        compiler_params=pltpu.CompilerParams(
            dimension_semantics=("arbitrary",), vmem_limit_bytes=VMEM_BIG),
        name="ffn_final" if final else "ffn",
    )(*args)


def _ret_body(x_ref, pos_ref, invf_ref, ng_ref, sh_ref, sc_ref, gt_ref,
              mask_ref, qd_ref, kd_ref, cd_ref, w_ref, wout_ref, o_ref,
              q_ref, k_ref, v_ref, g_ref, gated_ref, state_ref):
    @pl.when(pl.program_id(1) == 0)
    def _():
        state_ref[...] = jnp.zeros_like(state_ref)

    x = x_ref[...]
    h = _rms_mod(x, ng_ref[0], sh_ref[0], sc_ref[0]).astype(BF16)
    half = RET_QK_DIM // 2
    qk_w = RET_HEADS * RET_QK_DIM
    v_w = RET_HEADS * RET_V_DIM
    ang = pos_ref[...].astype(F32) * invf_ref[...]
    cos = jnp.cos(ang)
    sin = jnp.sin(ang)
    for hh in range(RET_HEADS):
        lo = hh * RET_V_DIM
        v_ref[:, lo:lo + RET_V_DIM] = jnp.dot(
            h, w_ref[:, 2 * qk_w + lo:2 * qk_w + lo + RET_V_DIM],
            preferred_element_type=F32).astype(BF16)
        g_ref[:, lo:lo + RET_V_DIM] = jnp.dot(
            h, w_ref[:, 2 * qk_w + v_w + lo:2 * qk_w + v_w + lo + RET_V_DIM],
            preferred_element_type=F32).astype(BF16)
    for base, dst, scale in ((0, q_ref, None), (qk_w, k_ref, RET_QK_DIM ** -0.5)):
        for hh in range(RET_HEADS):
            lo = hh * RET_QK_DIM
            t = jnp.dot(h, w_ref[:, base + lo:base + lo + RET_QK_DIM], preferred_element_type=F32)
            t1 = t[:, :half]
            t2 = t[:, half:]
            r1 = t1 * cos - t2 * sin
            r2 = t1 * sin + t2 * cos
            if scale is not None:
                r1 = r1 * scale
                r2 = r2 * scale
            dst[:, lo:lo + half] = r1.astype(BF16)
            dst[:, lo + half:lo + RET_QK_DIM] = r2.astype(BF16)

    for c in range(RET_TM // RET_CHUNK):
        rows = slice(c * RET_CHUNK, (c + 1) * RET_CHUNK)
        for hh in range(RET_HEADS):
            qh = q_ref[rows, hh * RET_QK_DIM:(hh + 1) * RET_QK_DIM]
            kh = k_ref[rows, hh * RET_QK_DIM:(hh + 1) * RET_QK_DIM]
            vh = v_ref[rows, hh * RET_V_DIM:(hh + 1) * RET_V_DIM]
            gh = g_ref[rows, hh * RET_V_DIM:(hh + 1) * RET_V_DIM].astype(F32)
            st = state_ref[hh]
            scores = lax.dot_general(qh, kh, (((1,), (1,)), ((), ())),
                                     preferred_element_type=F32) * mask_ref[hh]
            inner = jnp.dot(scores.astype(BF16), vh, preferred_element_type=F32)
            cross = jnp.dot(qh, st.astype(BF16), preferred_element_type=F32) * qd_ref[hh]
            o = inner + cross
            kdec = (kh.astype(F32) * kd_ref[hh]).astype(BF16)
            state_ref[hh] = st * cd_ref[hh] + lax.dot_general(
                kdec, vh, (((0,), (0,)), ((), ())), preferred_element_type=F32)
            mu = jnp.mean(o, axis=-1, keepdims=True)
            oc = o - mu
            var = jnp.mean(oc * oc, axis=-1, keepdims=True)
            on = oc * lax.rsqrt(var + GN_EPS)
            gated_ref[rows, hh * RET_V_DIM:(hh + 1) * RET_V_DIM] = (
                (gh * jax.nn.sigmoid(gh)) * on).astype(BF16)
    acc = jnp.dot(gated_ref[...], wout_ref[...], preferred_element_type=F32)
    o_ref[...] = x + gt_ref[0] * acc


def _ret(x2, pos2, inv_freq, ng3, mods3, norm_row, mod_row, gate_row, consts, w_in, w_out,
         bsz, tiles_per_batch):
    t, d = x2.shape
    qk_w = RET_HEADS * RET_QK_DIM
    v_w = RET_HEADS * RET_V_DIM
    mask, qd, kd, cd = consts
    tok = lambda w: pl.BlockSpec((RET_TM, w), lambda b, n: (b * tiles_per_batch + n, 0))
    full = lambda a: pl.BlockSpec(a.shape, lambda b, n: (0,) * a.ndim)
    return pl.pallas_call(
        _ret_body,
        grid=(bsz, tiles_per_batch),
        in_specs=[
            tok(d),
            tok(1),
            pl.BlockSpec((1, RET_QK_DIM // 2), lambda b, n: (0, 0)),
            _vec_spec(lambda b, n: norm_row),
            _vec_spec(lambda b, n: mod_row(b)),
            _vec_spec(lambda b, n: mod_row(b) + 1),
            _vec_spec(lambda b, n: gate_row(b)),
            full(mask), full(qd), full(kd), full(cd),
            _resident(w_in.shape),
            _resident(w_out.shape),
        ],
        out_specs=tok(d),
        out_shape=jax.ShapeDtypeStruct((t, d), F32),
        scratch_shapes=[pltpu.VMEM((RET_TM, qk_w), BF16), pltpu.VMEM((RET_TM, qk_w), BF16),
                        pltpu.VMEM((RET_TM, v_w), BF16), pltpu.VMEM((RET_TM, v_w), BF16),
                        pltpu.VMEM((RET_TM, v_w), BF16),
                        pltpu.VMEM((RET_HEADS, RET_QK_DIM, RET_V_DIM), F32)],
        compiler_params=pltpu.CompilerParams(
            dimension_semantics=("arbitrary", "arbitrary"), vmem_limit_bytes=VMEM_BIG),
        name="ret",
    )(x2, pos2, inv_freq, ng3, mods3, mods3, mods3, mask, qd, kd, cd, w_in, w_out)


def _ret_consts():
    hds, c = RET_HEADS, RET_CHUNK
    log_gamma = jnp.log(1.0 - jnp.power(2.0, -5.0 - jnp.arange(hds, dtype=F32)))
    pos = jnp.arange(c, dtype=F32)
    diff = pos[:, None] - pos[None, :]
    mask = jnp.where(diff >= 0, jnp.exp(log_gamma[:, None, None] * jnp.maximum(diff, 0.0)), 0.0)
    qd = jnp.exp(log_gamma[:, None] * (pos + 1.0))[:, :, None]
    kd = jnp.exp(log_gamma[:, None] * (c - 1.0 - pos))[:, :, None]
    cd = jnp.broadcast_to(jnp.exp(log_gamma * c)[:, None, None], (hds, 1, RET_V_DIM))
    return mask, qd, kd, cd


def _s5_pre_body(x_ref, ng_ref, sh_ref, sc_ref, o_ref):
    h = _rms_mod(x_ref[...], ng_ref[0], sh_ref[0], sc_ref[0])
    d = h.shape[1]
    for r in range(S5_TM // S5_CHUNK):
        ht = h[r * S5_CHUNK:(r + 1) * S5_CHUNK, :].T
        o_ref[:, r * SUBLANES:(r + 1) * SUBLANES, :] = ht.reshape(d // SUBLANES, SUBLANES, S5_CHUNK)


def _s5_pre(x2, ng3, mods3, norm_row, mod_row, tiles_per_batch):
    t, d = x2.shape
    rows = S5_TM // S5_CHUNK * SUBLANES
    return pl.pallas_call(
        _s5_pre_body,
        grid=(t // S5_TM,),
        in_specs=[pl.BlockSpec((S5_TM, d), lambda i: (i, 0)),
                  _vec_spec(lambda i: norm_row),
                  _vec_spec(lambda i: mod_row(i // tiles_per_batch)),
                  _vec_spec(lambda i: mod_row(i // tiles_per_batch) + 1)],
        out_specs=pl.BlockSpec((d // SUBLANES, rows, S5_CHUNK), lambda i: (0, i, 0)),
        out_shape=jax.ShapeDtypeStruct((d // SUBLANES, t // S5_CHUNK * SUBLANES, S5_CHUNK), F32),
        compiler_params=pltpu.CompilerParams(
            dimension_semantics=("arbitrary",), vmem_limit_bytes=VMEM_MID),
        name="s5_pre",
    )(x2, ng3, mods3, mods3)


def _pair(re, im):
    return jnp.concatenate([re, im], axis=1)


def _cmul2(x, xs, w1, w2):
    return x * w1 + xs * w2, xs * w1 - x * w2


def _s5_core_body(*refs, chunks_per_seq):
    kk, pp, tt = S5_GROUP, S5_STATE, S5_CHUNK
    n_half = kk // SUBLANES
    u_refs = refs[:n_half]
    are_ref, aim_ref, ldt_ref, btr_ref, bti_ref, cr_ref, ci_ref, d_ref = refs[n_half:n_half + 8]
    o_refs = refs[n_half + 8:2 * n_half + 8]
    kt_ref, toep_ref, ucat_ref, ws_ref, wo_ref = refs[2 * n_half + 8:]
    n_rows = u_refs[0].shape[0] // SUBLANES

    def chan(block_refs, k):
        return block_refs[k // SUBLANES], pl.ds(k % SUBLANES, n_rows, stride=SUBLANES)

    a_re = are_ref[0]
    a_im = aim_ref[0]
    dt = jnp.exp(ldt_ref[0])
    lam_re = dt * a_re
    lam_im = dt * a_im
    mag = jnp.exp(lam_re)
    n_tab = tt.bit_length() - 1
    n_scan = (chunks_per_seq - 1).bit_length()
    n_lvl = n_tab + n_scan
    sq = [(mag * jnp.cos(lam_im), mag * jnp.sin(lam_im))]
    for _ in range(n_lvl - 1):
        sq.append(_cmul(sq[-1][0], sq[-1][1], sq[-1][0], sq[-1][1]))
    lvl_re = jnp.concatenate([r for r, _ in sq], axis=0)
    lvl_im = jnp.concatenate([i for _, i in sq], axis=0)
    w1_all = _pair(lvl_re, lvl_re)
    w2_all = _pair(-lvl_im, lvl_im)
    w1 = lambda b: w1_all[b:b + 1, :]
    w2 = lambda b: w2_all[b:b + 1, :]

    def power_table(first, ascending):
        lane = lax.broadcasted_iota(jnp.int32, (SUBLANES, 2 * pp), 1)
        x = jnp.where(lane < pp, 1.0, 0.0).astype(F32)
        xs = 1.0 - x
        for b in range(SUBLANES.bit_length()):
            nx, nxs = _cmul2(x, xs, w1(b), w2(b))
            bit = ((first >> b) & 1) == 1
            x = jnp.where(bit, nx, x)
            xs = jnp.where(bit, nxs, xs)
        for b in range(SUBLANES.bit_length() - 1, n_tab):
            nx, nxs = _cmul2(x, xs, w1(b), w2(b))
            x = jnp.concatenate([x, nx] if ascending else [nx, x], axis=0)
            xs = jnp.concatenate([xs, nxs] if ascending else [nxs, xs], axis=0)
        return x, xs

    sub = lax.broadcasted_iota(jnp.int32, (SUBLANES, 2 * pp), 0)
    e0, _ = power_table(sub, True)
    e1, e1s = power_table(sub + 1, True)
    er, ers = power_table(SUBLANES - 1 - sub, False)

    nr, ni = sq[0][0] - 1.0, sq[0][1]
    den = a_re * a_re + a_im * a_im
    q_re = (nr * a_re + ni * a_im) / den
    q_im = (ni * a_re - nr * a_im) / den
    bbr, bbi = _cmul(q_re, q_im, btr_ref[0], bti_ref[0])
    bb = _pair(bbr, bbi)
    bbs = _pair(bbi, bbr)
    b1 = _pair(bbr, bbr)
    b2 = _pair(-bbi, bbi)
    c_re = cr_ref[0]
    c_im = ci_ref[0]
    c1 = _pair(c_re, -c_re)
    c2 = _pair(-c_im, -c_im)

    cb = jnp.concatenate([bb * c1[k:k + 1, :] + bbs * c2[k:k + 1, :] for k in range(kk)], axis=0)
    kt_ref[...] = lax.dot_general(cb, e0, (((1,), (1,)), ((), ())),
                                  precision=lax.Precision.HIGHEST, preferred_element_type=F32)

    causal = (lax.broadcasted_iota(jnp.int32, (tt, tt), 1)
              >= lax.broadcasted_iota(jnp.int32, (tt, tt), 0))

    def toep_rows(kp, carry):
        row0 = pl.multiple_of(kp * tt, tt)
        for k in range(kk):
            lag = kt_ref[pl.ds(k * kk + kp, 1), :]
            blk = pltpu.roll(jnp.broadcast_to(lag, (tt, tt)), 0, 1, stride=1, stride_axis=0)
            toep_ref[pl.ds(row0, tt), k * tt:(k + 1) * tt] = jnp.where(causal, blk, 0.0).astype(BF16)
        return carry

    lax.fori_loop(0, kk, toep_rows, 0)

    us = []
    for k in range(kk):
        flat, rows_k = chan(u_refs, k)
        us.append(flat[rows_k, :])
    for kp in range(kk):
        ucat_ref[:, kp * tt:(kp + 1) * tt] = us[kp].astype(BF16)
        ws_ref[kp * tt:(kp + 1) * tt, :] = (er * b1[kp:kp + 1, :] + ers * b2[kp:kp + 1, :]).astype(BF16)
    ucat = ucat_ref[...]

    s = jnp.dot(ucat, ws_ref[...], preferred_element_type=F32)
    ss = pltpu.roll(s, pp, 1)
    cidx = lax.broadcasted_iota(jnp.int32, (n_rows, 2 * pp), 0) % chunks_per_seq
    for lvl in range(n_scan):
        sh = 1 << lvl
        tx, txs = _cmul2(pltpu.roll(s, sh, 0), pltpu.roll(ss, sh, 0), w1(n_tab + lvl), w2(n_tab + lvl))
        keep = cidx >= sh
        s = s + jnp.where(keep, tx, 0.0)
        ss = ss + jnp.where(keep, txs, 0.0)
    prev = jnp.where(cidx >= 1, pltpu.roll(s, 1, 0), 0.0).astype(BF16)

    for k in range(kk):
        wo_ref[k * tt:(k + 1) * tt, :] = (e1 * c1[k:k + 1, :] + e1s * c2[k:k + 1, :]).astype(BF16)

    y = jnp.dot(ucat, toep_ref[...], preferred_element_type=F32)
    y = y + lax.dot_general(prev, wo_ref[...], (((1,), (1,)), ((), ())),
                            preferred_element_type=F32)
    for k in range(kk):
        z = y[:, k * tt:(k + 1) * tt] + d_ref[0, k:k + 1, :] * us[k]
        flat, rows_k = chan(o_refs, k)
        flat[rows_k, :] = jax.nn.gelu(z)


def _s5_core(ht, a_re, a_im, log_dt, bt_re, bt_im, c_re, c_im, d_b, chunks_per_seq):
    _, flat_rows, tt = ht.shape
    rows = flat_rows // SUBLANES
    g, kk, pp = S5_GROUPS, S5_GROUP, S5_STATE
    grp = lambda *shape: pl.BlockSpec((1,) + shape, lambda i: (i,) + (0,) * len(shape))
    n_half = kk // SUBLANES
    halves = [pl.BlockSpec((None, flat_rows, tt), lambda i, h=h: (i * n_half + h, 0, 0))
              for h in range(n_half)]
    return pl.pallas_call(
        functools.partial(_s5_core_body, chunks_per_seq=chunks_per_seq),
        grid=(g,),
        in_specs=halves + [grp(1, pp), grp(1, pp), grp(1, 1),
                           grp(kk, pp), grp(kk, pp), grp(kk, pp), grp(kk, pp),
                           grp(kk, tt)],
        out_specs=[pl.BlockSpec((None, flat_rows, tt), lambda i: (i, 0, 0))] * n_half,
        out_shape=[jax.ShapeDtypeStruct((g, flat_rows, tt), F32)] * n_half,
        scratch_shapes=[pltpu.VMEM((kk * kk, tt), F32),
                        pltpu.VMEM((kk * tt, kk * tt), BF16),
                        pltpu.VMEM((rows, kk * tt), BF16),
                        pltpu.VMEM((kk * tt, 2 * pp), BF16),
                        pltpu.VMEM((kk * tt, 2 * pp), BF16)],
        compiler_params=pltpu.CompilerParams(
            dimension_semantics=("arbitrary",), vmem_limit_bytes=VMEM_MID),
        name="s5_core",
    )(*([ht] * n_half), a_re, a_im, log_dt, bt_re, bt_im, c_re, c_im, d_b)


def _s5_post_body(*refs):
    n_half = S5_GROUP // SUBLANES
    y_refs = refs[:n_half]
    x_ref, gt_ref, w_ref, o_ref, ys_ref = refs[n_half:]
    for r in range(S5_TM // S5_CHUNK):
        yt = jnp.concatenate(
            [y_refs[h][:, r * SUBLANES:(r + 1) * SUBLANES, :].reshape(S5_GROUPS * SUBLANES, S5_CHUNK)
             for h in range(n_half)], axis=0)
        ys_ref[r * S5_CHUNK:(r + 1) * S5_CHUNK, :] = yt.T.astype(BF16)
    yb = ys_ref[...]
    d = x_ref.shape[1]
    y1 = jnp.dot(yb, w_ref[:, :d], preferred_element_type=F32)
    y2 = jnp.dot(yb, w_ref[:, d:], preferred_element_type=F32)
    o_ref[...] = x_ref[...] + gt_ref[0] * (y1 * jax.nn.sigmoid(y2))


def _s5_post(yts, x2, mods3, gate_row, w_glu, tiles_per_batch):
    t, d = x2.shape
    rows = S5_TM // S5_CHUNK
    return pl.pallas_call(
        _s5_post_body,
        grid=(t // S5_TM,),
        in_specs=[pl.BlockSpec((S5_GROUPS, rows * SUBLANES, S5_CHUNK),
                               lambda i: (0, i, 0))] * (S5_GROUP // SUBLANES) + [
                  pl.BlockSpec((S5_TM, d), lambda i: (i, 0)),
                  _vec_spec(lambda i: gate_row(i // tiles_per_batch)),
                  _resident(w_glu.shape)],
        out_specs=pl.BlockSpec((S5_TM, d), lambda i: (i, 0)),
        out_shape=jax.ShapeDtypeStruct((t, d), F32),
        scratch_shapes=[pltpu.VMEM((S5_TM, d), BF16)],
        compiler_params=pltpu.CompilerParams(
            dimension_semantics=("arbitrary",), vmem_limit_bytes=VMEM_BIG),
        name="s5_post",
    )(*yts, x2, mods3, w_glu)


def kernel(x, c, positions, ada_w, ada_b, norm_g, ffn_w_in, ffn_w_out, ret_w_in, ret_w_out,
           s5_a_re, s5_a_im, s5_b_re, s5_b_im, s5_c_re, s5_c_im, s5_d, s5_log_dt, s5_w_glu,
           final_g):
    bsz, seq, d = x.shape
    depth = ada_w.shape[0]
    assert d == D_MODEL and seq % S5_TM == 0 and seq % RET_CHUNK == 0
    t = bsz * seq
    g, kk, pp = S5_GROUPS, S5_GROUP, S5_STATE

    mods3 = _ada(c, ada_w, ada_b).reshape(depth * bsz * N_MOD, 1, d)
    ng3 = norm_g.reshape(depth * 3, 1, d)
    x2 = x.reshape(t, d)
    pos2 = positions.reshape(t, 1)
    half = RET_QK_DIM // 2
    inv_freq = (ROPE_THETA ** (-jnp.arange(half, dtype=F32) / half)).reshape(1, half)
    ret_consts = _ret_consts()

    for i in range(depth):
        mod_row = lambda s: (lambda b: (i * bsz + b) * N_MOD + 3 * s)
        gate_row = lambda s: (lambda b: (i * bsz + b) * N_MOD + 3 * s + 2)
        x2 = _ffn(x2, ng3, mods3, i * 3, mod_row(0), ffn_w_in[i, 0].astype(BF16),
                  ffn_w_out[i, 0].astype(BF16), seq // FFN_TM)
        j = i // 2
        if i % 2 == 0:
            x2 = _ret(x2, pos2, inv_freq, ng3, mods3, i * 3 + 1, mod_row(1), gate_row(1),
                      ret_consts, ret_w_in[j].astype(BF16), ret_w_out[j].astype(BF16),
                      bsz, seq // RET_TM)
        else:
            ht = _s5_pre(x2, ng3, mods3, i * 3 + 1, mod_row(1), seq // S5_TM)
            yts = _s5_core(
                ht,
                s5_a_re[j].reshape(g, 1, pp), s5_a_im[j].reshape(g, 1, pp),
                s5_log_dt[j].reshape(g, 1, 1),
                s5_b_re[j].transpose(0, 2, 1), s5_b_im[j].transpose(0, 2, 1),
                s5_c_re[j], s5_c_im[j],
                jnp.broadcast_to(s5_d[j].reshape(g, kk, 1), (g, kk, S5_CHUNK)),
                seq // S5_CHUNK)
            w_glu = s5_w_glu[j].reshape(g, kk // SUBLANES, SUBLANES, -1).transpose(1, 0, 2, 3)
            w_glu = w_glu.reshape(d, -1).astype(BF16)
            x2 = _s5_post(yts, x2, mods3, gate_row(1), w_glu, seq // S5_TM)
        x2 = _ffn(x2, ng3, mods3, i * 3 + 2, mod_row(2), ffn_w_in[i, 1].astype(BF16),
                  ffn_w_out[i, 1].astype(BF16), seq // FFN_TM,
                  final_g=final_g if i == depth - 1 else None)
    return x2.reshape(bsz, seq, d)
```

```python
import functools

import jax
import jax.numpy as jnp
from jax import lax
from jax.experimental import pallas as pl
from jax.experimental.pallas import tpu as pltpu

F32 = jnp.float32
BF16 = jnp.bfloat16

D_MODEL = 1024
N_MOD = 9
D_FF = 2816
RET_HEADS = 4
RET_QK_DIM = 256
RET_V_DIM = 512
RET_CHUNK = 256
ROPE_THETA = 10000.0
S5_GROUP = 16
S5_GROUPS = 64
S5_STATE = 64
NORM_EPS = 1e-6
GN_EPS = 1e-5

LANES = 128
SUBLANES = 8
MXU_DIM = 256
S5_CHUNK = LANES
VMEM_BIG = 56 * 1024 * 1024
VMEM_MID = 40 * 1024 * 1024

FFN_TM = 512
FFN_CHUNKS = ((0, 6 * MXU_DIM), (6 * MXU_DIM, D_FF))
assert D_FF % MXU_DIM == 0
RET_TM = 512
ADA_TN = 1152
S5_TM = 1024


def _rms_mod(x, g, shift, scale):
    y = x * lax.rsqrt(jnp.mean(x * x, axis=-1, keepdims=True) + NORM_EPS)
    return (y * g) * (1.0 + scale) + shift


def _cmul(ar, ai, br, bi):
    return ar * br - ai * bi, ar * bi + ai * br


def _resident(shape):
    zeros = (0,) * len(shape)
    return pl.BlockSpec(shape, lambda *_: zeros, pipeline_mode=pl.Buffered(1))


def _ada_body(ct_ref, w_ref, b_ref, o_ref):
    ct = ct_ref[...]
    ca = ct * jax.nn.sigmoid(ct)
    w = w_ref[0]
    for b in range(ct.shape[1]):
        o_ref[0, b:b + 1, :] = jnp.sum(w * ca[:, b:b + 1], axis=0, keepdims=True) + b_ref[0]


def _ada(c, ada_w, ada_b):
    depth, d, n = ada_w.shape
    bsz = c.shape[0]
    return pl.pallas_call(
        _ada_body,
        grid=(depth, n // ADA_TN),
        in_specs=[
            pl.BlockSpec((d, bsz), lambda i, j: (0, 0)),
            pl.BlockSpec((1, d, ADA_TN), lambda i, j: (i, 0, j)),
            pl.BlockSpec((1, 1, ADA_TN), lambda i, j: (i, 0, j)),
        ],
        out_specs=pl.BlockSpec((1, bsz, ADA_TN), lambda i, j: (i, 0, j)),
        out_shape=jax.ShapeDtypeStruct((depth, bsz, n), F32),
        compiler_params=pltpu.CompilerParams(
            dimension_semantics=("arbitrary", "arbitrary"), vmem_limit_bytes=VMEM_MID),
        name="ada",
    )(c.T, ada_w, ada_b.reshape(depth, 1, n))


def _ffn_body(x_ref, ng_ref, sh_ref, sc_ref, gt_ref, win_ref, wout_ref, *rest, final):
    if final:
        fg_ref, o_ref = rest
    else:
        (o_ref,) = rest
    x = x_ref[...]
    h = _rms_mod(x, ng_ref[0], sh_ref[0], sc_ref[0]).astype(BF16)
    acc = None
    for lo, hi in FFN_CHUNKS:
        g = jnp.dot(h, win_ref[:, lo:hi], preferred_element_type=F32)
        u = jnp.dot(h, win_ref[:, D_FF + lo:D_FF + hi], preferred_element_type=F32)
        a = (g * jax.nn.sigmoid(g) * u).astype(BF16)
        p = jnp.dot(a, wout_ref[lo:hi, :], preferred_element_type=F32)
        acc = p if acc is None else acc + p
    y = x + (0.5 * gt_ref[0]) * acc
    if final:
        y = (y * lax.rsqrt(jnp.mean(y * y, axis=-1, keepdims=True) + NORM_EPS)) * fg_ref[...]
    o_ref[...] = y


def _vec_spec(row_fn):
    return pl.BlockSpec((1, 1, D_MODEL), lambda *idx: (row_fn(*idx), 0, 0))


def _ffn(x2, ng3, mods3, norm_row, mod_row, w_in, w_out, tiles_per_batch, final_g=None):
    t, d = x2.shape
    final = final_g is not None
    in_specs = [
        pl.BlockSpec((FFN_TM, d), lambda i: (i, 0)),
        _vec_spec(lambda i: norm_row),
        _vec_spec(lambda i: mod_row(i // tiles_per_batch)),
        _vec_spec(lambda i: mod_row(i // tiles_per_batch) + 1),
        _vec_spec(lambda i: mod_row(i // tiles_per_batch) + 2),
        _resident(w_in.shape),
        _resident(w_out.shape),
    ]
    args = [x2, ng3, mods3, mods3, mods3, w_in, w_out]
    if final:
        in_specs.append(pl.BlockSpec((1, d), lambda i: (0, 0)))
        args.append(final_g.reshape(1, d))
    return pl.pallas_call(
        functools.partial(_ffn_body, final=final),
        grid=(t // FFN_TM,),
        in_specs=in_specs,
        out_specs=pl.BlockSpec((FFN_TM, d), lambda i: (i, 0)),
        out_shape=jax.ShapeDtypeStruct((t, d), F32),
        compiler_params=pltpu.CompilerParams(
            dimension_semantics=("arbitrary",), vmem_limit_bytes=VMEM_BIG),
        name="ffn_final" if final else "ffn",
    )(*args)


def _ret_body(x_ref, pos_ref, invf_ref, ng_ref, sh_ref, sc_ref, gt_ref,
              mask_ref, qd_ref, kd_ref, cd_ref, w_ref, wout_ref, o_ref,
              q_ref, k_ref, v_ref, g_ref, gated_ref, state_ref):
    @pl.when(pl.program_id(1) == 0)
    def _():
        state_ref[...] = jnp.zeros_like(state_ref)

    x = x_ref[...]
    h = _rms_mod(x, ng_ref[0], sh_ref[0], sc_ref[0]).astype(BF16)
    half = RET_QK_DIM // 2
    qk_w = RET_HEADS * RET_QK_DIM
    v_w = RET_HEADS * RET_V_DIM
    ang = pos_ref[...].astype(F32) * invf_ref[...]
    cos = jnp.cos(ang)
    sin = jnp.sin(ang)
    for hh in range(RET_HEADS):
        lo = hh * RET_V_DIM
        v_ref[:, lo:lo + RET_V_DIM] = jnp.dot(
            h, w_ref[:, 2 * qk_w + lo:2 * qk_w + lo + RET_V_DIM],
            preferred_element_type=F32).astype(BF16)
        g_ref[:, lo:lo + RET_V_DIM] = jnp.dot(
            h, w_ref[:, 2 * qk_w + v_w + lo:2 * qk_w + v_w + lo + RET_V_DIM],
            preferred_element_type=F32).astype(BF16)
    for base, dst, scale in ((0, q_ref, None), (qk_w, k_ref, RET_QK_DIM ** -0.5)):
        for hh in range(RET_HEADS):
            lo = hh * RET_QK_DIM
            t = jnp.dot(h, w_ref[:, base + lo:base + lo + RET_QK_DIM], preferred_element_type=F32)
            t1 = t[:, :half]
            t2 = t[:, half:]
            r1 = t1 * cos - t2 * sin
            r2 = t1 * sin + t2 * cos
            if scale is not None:
                r1 = r1 * scale
                r2 = r2 * scale
            dst[:, lo:lo + half] = r1.astype(BF16)
            dst[:, lo + half:lo + RET_QK_DIM] = r2.astype(BF16)

    for c in range(RET_TM // RET_CHUNK):
        rows = slice(c * RET_CHUNK, (c + 1) * RET_CHUNK)
        heads = range(RET_HEADS)
        qs = [q_ref[rows, hh * RET_QK_DIM:(hh + 1) * RET_QK_DIM] for hh in heads]
        ks = [k_ref[rows, hh * RET_QK_DIM:(hh + 1) * RET_QK_DIM] for hh in heads]
        vs = [v_ref[rows, hh * RET_V_DIM:(hh + 1) * RET_V_DIM] for hh in heads]
        scores = [lax.dot_general(qs[hh], ks[hh], (((1,), (1,)), ((), ())),
                                  preferred_element_type=F32) for hh in heads]
        updates = [lax.dot_general((ks[hh].astype(F32) * kd_ref[hh]).astype(BF16), vs[hh],
                                   (((0,), (0,)), ((), ())), preferred_element_type=F32)
                   for hh in heads]
        crosses = []
        for hh in heads:
            st = state_ref[hh]
            crosses.append(jnp.dot(qs[hh], st.astype(BF16), preferred_element_type=F32))
            state_ref[hh] = st * cd_ref[hh] + updates[hh]
        for hh in heads:
            gh = g_ref[rows, hh * RET_V_DIM:(hh + 1) * RET_V_DIM].astype(F32)
            inner = jnp.dot((scores[hh] * mask_ref[hh]).astype(BF16), vs[hh],
                            preferred_element_type=F32)
            o = inner + crosses[hh] * qd_ref[hh]
            mu = jnp.mean(o, axis=-1, keepdims=True)
            oc = o - mu
            var = jnp.mean(oc * oc, axis=-1, keepdims=True)
            on = oc * lax.rsqrt(var + GN_EPS)
            gated_ref[rows, hh * RET_V_DIM:(hh + 1) * RET_V_DIM] = (
                (gh * jax.nn.sigmoid(gh)) * on).astype(BF16)
    acc = jnp.dot(gated_ref[...], wout_ref[...], preferred_element_type=F32)
    o_ref[...] = x + gt_ref[0] * acc


def _ret(x2, pos2, inv_freq, ng3, mods3, norm_row, mod_row, gate_row, consts, w_in, w_out,
         bsz, tiles_per_batch):
    t, d = x2.shape
    qk_w = RET_HEADS * RET_QK_DIM
    v_w = RET_HEADS * RET_V_DIM
    mask, qd, kd, cd = consts
    tok = lambda w: pl.BlockSpec((RET_TM, w), lambda b, n: (b * tiles_per_batch + n, 0))
    full = lambda a: pl.BlockSpec(a.shape, lambda b, n: (0,) * a.ndim)
    return pl.pallas_call(
        _ret_body,
        grid=(bsz, tiles_per_batch),
        in_specs=[
            tok(d),
            tok(1),
            pl.BlockSpec((1, RET_QK_DIM // 2), lambda b, n: (0, 0)),
            _vec_spec(lambda b, n: norm_row),
            _vec_spec(lambda b, n: mod_row(b)),
            _vec_spec(lambda b, n: mod_row(b) + 1),
            _vec_spec(lambda b, n: gate_row(b)),
            full(mask), full(qd), full(kd), full(cd),
            _resident(w_in.shape),
            _resident(w_out.shape),
        ],
        out_specs=tok(d),
        out_shape=jax.ShapeDtypeStruct((t, d), F32),
        scratch_shapes=[pltpu.VMEM((RET_TM, qk_w), BF16), pltpu.VMEM((RET_TM, qk_w), BF16),
                        pltpu.VMEM((RET_TM, v_w), BF16), pltpu.VMEM((RET_TM, v_w), BF16),
                        pltpu.VMEM((RET_TM, v_w), BF16),
                        pltpu.VMEM((RET_HEADS, RET_QK_DIM, RET_V_DIM), F32)],
        compiler_params=pltpu.CompilerParams(
            dimension_semantics=("arbitrary", "arbitrary"), vmem_limit_bytes=VMEM_BIG),
        name="ret",
    )(x2, pos2, inv_freq, ng3, mods3, mods3, mods3, mask, qd, kd, cd, w_in, w_out)


def _ret_consts():
    hds, c = RET_HEADS, RET_CHUNK
    log_gamma = jnp.log(1.0 - jnp.power(2.0, -5.0 - jnp.arange(hds, dtype=F32)))
    pos = jnp.arange(c, dtype=F32)
    diff = pos[:, None] - pos[None, :]
    mask = jnp.where(diff >= 0, jnp.exp(log_gamma[:, None, None] * jnp.maximum(diff, 0.0)), 0.0)
    qd = jnp.exp(log_gamma[:, None] * (pos + 1.0))[:, :, None]
    kd = jnp.exp(log_gamma[:, None] * (c - 1.0 - pos))[:, :, None]
    cd = jnp.broadcast_to(jnp.exp(log_gamma * c)[:, None, None], (hds, 1, RET_V_DIM))
    return mask, qd, kd, cd


def _s5_pre_body(x_ref, ng_ref, sh_ref, sc_ref, o_ref):
    h = _rms_mod(x_ref[...], ng_ref[0], sh_ref[0], sc_ref[0])
    d = h.shape[1]
    for r in range(S5_TM // S5_CHUNK):
        ht = h[r * S5_CHUNK:(r + 1) * S5_CHUNK, :].T
        o_ref[:, r * SUBLANES:(r + 1) * SUBLANES, :] = ht.reshape(d // SUBLANES, SUBLANES, S5_CHUNK)


def _s5_pre(x2, ng3, mods3, norm_row, mod_row, tiles_per_batch):
    t, d = x2.shape
    rows = S5_TM // S5_CHUNK * SUBLANES
    return pl.pallas_call(
        _s5_pre_body,
        grid=(t // S5_TM,),
        in_specs=[pl.BlockSpec((S5_TM, d), lambda i: (i, 0)),
                  _vec_spec(lambda i: norm_row),
                  _vec_spec(lambda i: mod_row(i // tiles_per_batch)),
                  _vec_spec(lambda i: mod_row(i // tiles_per_batch) + 1)],
        out_specs=pl.BlockSpec((d // SUBLANES, rows, S5_CHUNK), lambda i: (0, i, 0)),
        out_shape=jax.ShapeDtypeStruct((d // SUBLANES, t // S5_CHUNK * SUBLANES, S5_CHUNK), F32),
        compiler_params=pltpu.CompilerParams(
            dimension_semantics=("arbitrary",), vmem_limit_bytes=VMEM_MID),
        name="s5_pre",
    )(x2, ng3, mods3, mods3)


def _pair(re, im):
    return jnp.concatenate([re, im], axis=1)


def _cmul2(x, xs, w1, w2):
    return x * w1 + xs * w2, xs * w1 - x * w2


def _s5_levels(chunks_per_seq):
    return S5_CHUNK.bit_length() - 1, (chunks_per_seq - 1).bit_length()


def _s5_build(are_ref, aim_ref, ldt_ref, btr_ref, bti_ref, cr_ref, ci_ref,
              toep_ref, ws_ref, wo_ref, w1_ref, w2_ref, *, chunks_per_seq):
    kk, pp, tt = S5_GROUP, S5_STATE, S5_CHUNK
    a_re = are_ref[0]
    a_im = aim_ref[0]
    dt = jnp.exp(ldt_ref[0])
    lam_re = dt * a_re
    lam_im = dt * a_im
    mag = jnp.exp(lam_re)
    n_tab, n_scan = _s5_levels(chunks_per_seq)
    n_lvl = n_tab + n_scan
    sq = [(mag * jnp.cos(lam_im), mag * jnp.sin(lam_im))]
    for _ in range(n_lvl - 1):
        sq.append(_cmul(sq[-1][0], sq[-1][1], sq[-1][0], sq[-1][1]))
    lvl_re = jnp.concatenate([r for r, _ in sq], axis=0)
    lvl_im = jnp.concatenate([i for _, i in sq], axis=0)
    w1_all = _pair(lvl_re, lvl_re)
    w2_all = _pair(-lvl_im, lvl_im)
    w1 = lambda b: w1_all[b:b + 1, :]
    w2 = lambda b: w2_all[b:b + 1, :]

    def power_table(first, ascending):
        lane = lax.broadcasted_iota(jnp.int32, (SUBLANES, 2 * pp), 1)
        x = jnp.where(lane < pp, 1.0, 0.0).astype(F32)
        xs = 1.0 - x
        for b in range(SUBLANES.bit_length()):
            nx, nxs = _cmul2(x, xs, w1(b), w2(b))
            bit = ((first >> b) & 1) == 1
            x = jnp.where(bit, nx, x)
            xs = jnp.where(bit, nxs, xs)
        for b in range(SUBLANES.bit_length() - 1, n_tab):
            nx, nxs = _cmul2(x, xs, w1(b), w2(b))
            x = jnp.concatenate([x, nx] if ascending else [nx, x], axis=0)
            xs = jnp.concatenate([xs, nxs] if ascending else [nxs, xs], axis=0)
        return x, xs

    sub = lax.broadcasted_iota(jnp.int32, (SUBLANES, 2 * pp), 0)
    e0, _ = power_table(sub, True)
    e1, e1s = power_table(sub + 1, True)
    er, ers = power_table(SUBLANES - 1 - sub, False)

    nr, ni = sq[0][0] - 1.0, sq[0][1]
    den = a_re * a_re + a_im * a_im
    q_re = (nr * a_re + ni * a_im) / den
    q_im = (ni * a_re - nr * a_im) / den
    bbr, bbi = _cmul(q_re, q_im, btr_ref[0], bti_ref[0])
    bb = _pair(bbr, bbi)
    bbs = _pair(bbi, bbr)
    b1 = _pair(bbr, bbr)
    b2 = _pair(-bbi, bbi)
    c_re = cr_ref[0]
    c_im = ci_ref[0]
    c1 = _pair(c_re, -c_re)
    c2 = _pair(-c_im, -c_im)

    cb = jnp.concatenate([bb * c1[k:k + 1, :] + bbs * c2[k:k + 1, :] for k in range(kk)], axis=0)
    kt = lax.dot_general(cb, e0, (((1,), (1,)), ((), ())),
                         precision=lax.Precision.HIGHEST, preferred_element_type=F32)

    causal = (lax.broadcasted_iota(jnp.int32, (tt, tt), 1)
              >= lax.broadcasted_iota(jnp.int32, (tt, tt), 0))
    for kp in range(kk):
        for k in range(kk):
            lag = kt[k * kk + kp:k * kk + kp + 1, :]
            blk = pltpu.roll(jnp.broadcast_to(lag, (tt, tt)), 0, 1, stride=1, stride_axis=0)
            toep_ref[kp * tt:(kp + 1) * tt, k * tt:(k + 1) * tt] = (
                jnp.where(causal, blk, 0.0).astype(BF16))
    for kp in range(kk):
        ws_ref[kp * tt:(kp + 1) * tt, :] = (er * b1[kp:kp + 1, :] + ers * b2[kp:kp + 1, :]).astype(BF16)
    for k in range(kk):
        wo_ref[k * tt:(k + 1) * tt, :] = (e1 * c1[k:k + 1, :] + e1s * c2[k:k + 1, :]).astype(BF16)
    w1_ref[...] = w1_all
    w2_ref[...] = w2_all


def _s5_apply(u_refs, d_ref, o_refs, toep_ref, ws_ref, wo_ref, w1_ref, w2_ref, ucat_ref, *,
              chunks_per_seq):
    kk, pp, tt = S5_GROUP, S5_STATE, S5_CHUNK
    n_tab, n_scan = _s5_levels(chunks_per_seq)
    n_rows = u_refs[0].shape[0] // SUBLANES

    def chan(block_refs, k):
        return block_refs[k // SUBLANES], pl.ds(k % SUBLANES, n_rows, stride=SUBLANES)

    us = []
    for k in range(kk):
        flat, rows_k = chan(u_refs, k)
        us.append(flat[rows_k, :])
    for kp in range(kk):
        ucat_ref[:, kp * tt:(kp + 1) * tt] = us[kp].astype(BF16)
    ucat = ucat_ref[...]

    s = jnp.dot(ucat, ws_ref[...], preferred_element_type=F32)
    ss = pltpu.roll(s, pp, 1)
    cidx = lax.broadcasted_iota(jnp.int32, (n_rows, 2 * pp), 0) % chunks_per_seq
    for lvl in range(n_scan):
        sh = 1 << lvl
        row = slice(n_tab + lvl, n_tab + lvl + 1)
        tx, txs = _cmul2(pltpu.roll(s, sh, 0), pltpu.roll(ss, sh, 0), w1_ref[row, :], w2_ref[row, :])
        keep = cidx >= sh
        s = s + jnp.where(keep, tx, 0.0)
        ss = ss + jnp.where(keep, txs, 0.0)
    prev = jnp.where(cidx >= 1, pltpu.roll(s, 1, 0), 0.0).astype(BF16)

    y = jnp.dot(ucat, toep_ref[...], preferred_element_type=F32)
    y = y + lax.dot_general(prev, wo_ref[...], (((1,), (1,)), ((), ())),
                            preferred_element_type=F32)
    for k in range(kk):
        z = y[:, k * tt:(k + 1) * tt] + d_ref[0, k:k + 1, :] * us[k]
        flat, rows_k = chan(o_refs, k)
        flat[rows_k, :] = jax.nn.gelu(z)


def _s5_core_body(*refs, chunks_per_seq):
    n_half = S5_GROUP // SUBLANES
    u_refs = refs[:n_half]
    param_refs = refs[n_half:n_half + 7]
    d_ref = refs[n_half + 7]
    o_refs = refs[n_half + 8:2 * n_half + 8]
    slot_refs = refs[2 * n_half + 8:2 * n_half + 18]
    ucat_ref = refs[2 * n_half + 18]
    slots = (slot_refs[:5], slot_refs[5:])
    step = pl.program_id(0)

    @pl.when(step == 0)
    def _():
        for ref in slots[1]:
            ref[...] = jnp.zeros_like(ref)

    for parity in range(2):
        @pl.when(step % 2 == parity)
        def _():
            _s5_build(*param_refs, *slots[parity], chunks_per_seq=chunks_per_seq)
            _s5_apply(u_refs, d_ref, o_refs, *slots[1 - parity], ucat_ref,
                      chunks_per_seq=chunks_per_seq)


def _s5_core(ht, a_re, a_im, log_dt, bt_re, bt_im, c_re, c_im, d_b, chunks_per_seq):
    _, flat_rows, tt = ht.shape
    rows = flat_rows // SUBLANES
    g, kk, pp = S5_GROUPS, S5_GROUP, S5_STATE
    n_lvl = sum(_s5_levels(chunks_per_seq))
    built = lambda s: jnp.minimum(s, g - 1)
    applied = lambda s: jnp.maximum(s - 1, 0)
    par = lambda *shape: pl.BlockSpec((1,) + shape, lambda s: (built(s),) + (0,) * len(shape))
    n_half = kk // SUBLANES
    halves = [pl.BlockSpec((None, flat_rows, tt), lambda s, h=h: (applied(s) * n_half + h, 0, 0))
              for h in range(n_half)]
    slot = [pltpu.VMEM((kk * tt, kk * tt), BF16),
            pltpu.VMEM((kk * tt, 2 * pp), BF16), pltpu.VMEM((kk * tt, 2 * pp), BF16),
            pltpu.VMEM((n_lvl, 2 * pp), F32), pltpu.VMEM((n_lvl, 2 * pp), F32)]
    return pl.pallas_call(
        functools.partial(_s5_core_body, chunks_per_seq=chunks_per_seq),
        grid=(g + 1,),
        in_specs=halves + [par(1, pp), par(1, pp), par(1, 1),
                           par(kk, pp), par(kk, pp), par(kk, pp), par(kk, pp),
                           pl.BlockSpec((1, kk, tt), lambda s: (applied(s), 0, 0))],
        out_specs=[pl.BlockSpec((None, flat_rows, tt), lambda s: (applied(s), 0, 0))] * n_half,
        out_shape=[jax.ShapeDtypeStruct((g, flat_rows, tt), F32)] * n_half,
        scratch_shapes=slot + slot + [pltpu.VMEM((rows, kk * tt), BF16)],
        compiler_params=pltpu.CompilerParams(
            dimension_semantics=("arbitrary",), vmem_limit_bytes=VMEM_BIG),
        name="s5_core",
    )(*([ht] * n_half), a_re, a_im, log_dt, bt_re, bt_im, c_re, c_im, d_b)


def _s5_post_body(*refs):
    n_half = S5_GROUP // SUBLANES
    y_refs = refs[:n_half]
    x_ref, gt_ref, w_ref, o_ref, ys_ref = refs[n_half:]
    for r in range(S5_TM // S5_CHUNK):
        yt = jnp.concatenate(
            [y_refs[h][:, r * SUBLANES:(r + 1) * SUBLANES, :].reshape(S5_GROUPS * SUBLANES, S5_CHUNK)
             for h in range(n_half)], axis=0)
        ys_ref[r * S5_CHUNK:(r + 1) * S5_CHUNK, :] = yt.T.astype(BF16)
    yb = ys_ref[...]
    d = x_ref.shape[1]
    y1 = jnp.dot(yb, w_ref[:, :d], preferred_element_type=F32)
    y2 = jnp.dot(yb, w_ref[:, d:], preferred_element_type=F32)
    o_ref[...] = x_ref[...] + gt_ref[0] * (y1 * jax.nn.sigmoid(y2))


def _s5_post(yts, x2, mods3, gate_row, w_glu, tiles_per_batch):
    t, d = x2.shape
    rows = S5_TM // S5_CHUNK
    return pl.pallas_call(
        _s5_post_body,
        grid=(t // S5_TM,),
        in_specs=[pl.BlockSpec((S5_GROUPS, rows * SUBLANES, S5_CHUNK),
                               lambda i: (0, i, 0))] * (S5_GROUP // SUBLANES) + [
                  pl.BlockSpec((S5_TM, d), lambda i: (i, 0)),
                  _vec_spec(lambda i: gate_row(i // tiles_per_batch)),
                  _resident(w_glu.shape)],
        out_specs=pl.BlockSpec((S5_TM, d), lambda i: (i, 0)),
        out_shape=jax.ShapeDtypeStruct((t, d), F32),
        scratch_shapes=[pltpu.VMEM((S5_TM, d), BF16)],
        compiler_params=pltpu.CompilerParams(
            dimension_semantics=("arbitrary",), vmem_limit_bytes=VMEM_BIG),
        name="s5_post",
    )(*yts, x2, mods3, w_glu)


def kernel(x, c, positions, ada_w, ada_b, norm_g, ffn_w_in, ffn_w_out, ret_w_in, ret_w_out,
           s5_a_re, s5_a_im, s5_b_re, s5_b_im, s5_c_re, s5_c_im, s5_d, s5_log_dt, s5_w_glu,
           final_g):
    bsz, seq, d = x.shape
    depth = ada_w.shape[0]
    assert d == D_MODEL and seq % S5_TM == 0 and seq % RET_TM == 0 and RET_TM % RET_CHUNK == 0
    t = bsz * seq
    g, kk, pp = S5_GROUPS, S5_GROUP, S5_STATE

    mods3 = _ada(c, ada_w, ada_b).reshape(depth * bsz * N_MOD, 1, d)
    ng3 = norm_g.reshape(depth * 3, 1, d)
    x2 = x.reshape(t, d)
    pos2 = positions.reshape(t, 1)
    half = RET_QK_DIM // 2
    inv_freq = (ROPE_THETA ** (-jnp.arange(half, dtype=F32) / half)).reshape(1, half)
    ret_consts = _ret_consts()

    for i in range(depth):
        mod_row = lambda s: (lambda b: (i * bsz + b) * N_MOD + 3 * s)
        gate_row = lambda s: (lambda b: (i * bsz + b) * N_MOD + 3 * s + 2)
        x2 = _ffn(x2, ng3, mods3, i * 3, mod_row(0), ffn_w_in[i, 0].astype(BF16),
                  ffn_w_out[i, 0].astype(BF16), seq // FFN_TM)
        j = i // 2
        if i % 2 == 0:
            x2 = _ret(x2, pos2, inv_freq, ng3, mods3, i * 3 + 1, mod_row(1), gate_row(1),
                      ret_consts, ret_w_in[j].astype(BF16), ret_w_out[j].astype(BF16),
                      bsz, seq // RET_TM)
        else:
            ht = _s5_pre(x2, ng3, mods3, i * 3 + 1, mod_row(1), seq // S5_TM)
            yts = _s5_core(
                ht,
                s5_a_re[j].reshape(g, 1, pp), s5_a_im[j].reshape(g, 1, pp),
                s5_log_dt[j].reshape(g, 1, 1),
                s5_b_re[j].transpose(0, 2, 1), s5_b_im[j].transpose(0, 2, 1),
                s5_c_re[j], s5_c_im[j],
                jnp.broadcast_to(s5_d[j].reshape(g, kk, 1), (g, kk, S5_CHUNK)),
                seq // S5_CHUNK)
            w_glu = s5_w_glu[j].reshape(g, kk // SUBLANES, SUBLANES, -1).transpose(1, 0, 2, 3)
            w_glu = w_glu.reshape(d, -1).astype(BF16)
            x2 = _s5_post(yts, x2, mods3, gate_row(1), w_glu, seq // S5_TM)
        x2 = _ffn(x2, ng3, mods3, i * 3 + 2, mod_row(2), ffn_w_in[i, 1].astype(BF16),
                  ffn_w_out[i, 1].astype(BF16), seq // FFN_TM,
                  final_g=final_g if i == depth - 1 else None)
    return x2.reshape(bsz, seq, d)
```

```python
import functools

import jax
import jax.numpy as jnp
from jax import lax
from jax.experimental import pallas as pl
from jax.experimental.pallas import tpu as pltpu

F32 = jnp.float32
BF16 = jnp.bfloat16

D_MODEL = 1024
N_MOD = 9
D_FF = 2816
RET_HEADS = 4
RET_QK_DIM = 256
RET_V_DIM = 512
RET_CHUNK = 256
ROPE_THETA = 10000.0
S5_GROUP = 16
S5_GROUPS = 64
S5_STATE = 64
NORM_EPS = 1e-6
GN_EPS = 1e-5

LANES = 128
SUBLANES = 8
MXU_DIM = 256
S5_CHUNK = LANES
VMEM_BIG = 56 * 1024 * 1024
VMEM_MID = 40 * 1024 * 1024

FFN_TM = 1024
FFN_CHUNKS = ((0, 4 * MXU_DIM), (4 * MXU_DIM, 8 * MXU_DIM), (8 * MXU_DIM, D_FF))
assert D_FF % MXU_DIM == 0
RET_TM = 512
ADA_TN = 1152
S5_TM = 1024


def _rms_mod(x, g, shift, scale):
    y = x * lax.rsqrt(jnp.mean(x * x, axis=-1, keepdims=True) + NORM_EPS)
    return (y * g) * (1.0 + scale) + shift


def _cmul(ar, ai, br, bi):
    return ar * br - ai * bi, ar * bi + ai * br


def _resident(shape):
    zeros = (0,) * len(shape)
    return pl.BlockSpec(shape, lambda *_: zeros, pipeline_mode=pl.Buffered(1))


def _ada_body(ct_ref, w_ref, b_ref, o_ref):
    ct = ct_ref[...]
    ca = ct * jax.nn.sigmoid(ct)
    w = w_ref[0]
    for b in range(ct.shape[1]):
        o_ref[0, b:b + 1, :] = jnp.sum(w * ca[:, b:b + 1], axis=0, keepdims=True) + b_ref[0]


def _ada(c, ada_w, ada_b):
    depth, d, n = ada_w.shape
    bsz = c.shape[0]
    return pl.pallas_call(
        _ada_body,
        grid=(depth, n // ADA_TN),
        in_specs=[
            pl.BlockSpec((d, bsz), lambda i, j: (0, 0)),
            pl.BlockSpec((1, d, ADA_TN), lambda i, j: (i, 0, j)),
            pl.BlockSpec((1, 1, ADA_TN), lambda i, j: (i, 0, j)),
        ],
        out_specs=pl.BlockSpec((1, bsz, ADA_TN), lambda i, j: (i, 0, j)),
        out_shape=jax.ShapeDtypeStruct((depth, bsz, n), F32),
        compiler_params=pltpu.CompilerParams(
            dimension_semantics=("arbitrary", "arbitrary"), vmem_limit_bytes=VMEM_MID),
        name="ada",
    )(c.T, ada_w, ada_b.reshape(depth, 1, n))


def _ffn_body(x_ref, ng_ref, sh_ref, sc_ref, gt_ref, win_ref, wout_ref, *rest, final, n_cast):
    if final:
        fg_ref, rest = rest[0], rest[1:]
    src_refs, o_ref, dst_refs = rest[:n_cast], rest[n_cast], rest[n_cast + 1:]
    for src, dst in zip(src_refs, dst_refs):
        dst[...] = src[...].astype(BF16)
    x = x_ref[...]
    h = _rms_mod(x, ng_ref[0], sh_ref[0], sc_ref[0]).astype(BF16)
    acc = None
    for lo, hi in FFN_CHUNKS:
        g = jnp.dot(h, win_ref[:, lo:hi], preferred_element_type=F32)
        u = jnp.dot(h, win_ref[:, D_FF + lo:D_FF + hi], preferred_element_type=F32)
        a = (g * jax.nn.sigmoid(g) * u).astype(BF16)
        p = jnp.dot(a, wout_ref[lo:hi, :], preferred_element_type=F32)
        acc = p if acc is None else acc + p
    y = x + (0.5 * gt_ref[0]) * acc
    if final:
        y = (y * lax.rsqrt(jnp.mean(y * y, axis=-1, keepdims=True) + NORM_EPS)) * fg_ref[...]
    o_ref[...] = y


def _vec_spec(row_fn):
    return pl.BlockSpec((1, 1, D_MODEL), lambda *idx: (row_fn(*idx), 0, 0))


def _ffn(x2, ng3, mods3, norm_row, mod_row, w_in, w_out, tiles_per_batch, final_g=None, cast=()):
    t, d = x2.shape
    n_steps = t // FFN_TM
    final = final_g is not None
    in_specs = [
        pl.BlockSpec((FFN_TM, d), lambda i: (i, 0)),
        _vec_spec(lambda i: norm_row),
        _vec_spec(lambda i: mod_row(i // tiles_per_batch)),
        _vec_spec(lambda i: mod_row(i // tiles_per_batch) + 1),
        _vec_spec(lambda i: mod_row(i // tiles_per_batch) + 2),
        _resident(w_in.shape),
        _resident(w_out.shape),
    ]
    args = [x2, ng3, mods3, mods3, mods3, w_in, w_out]
    if final:
        in_specs.append(pl.BlockSpec((1, d), lambda i: (0, 0)))
        args.append(final_g.reshape(1, d))
    out_specs = [pl.BlockSpec((FFN_TM, d), lambda i: (i, 0))]
    out_shape = [jax.ShapeDtypeStruct((t, d), F32)]
    for arr, lead in cast:
        r, c = arr.shape[len(lead):]
        rb = r // n_steps
        assert rb * n_steps == r and rb % (2 * SUBLANES) == 0, (r, n_steps)
        in_specs.append(pl.BlockSpec((None,) * len(lead) + (rb, c),
                                     lambda i, lead=lead: tuple(lead) + (i, 0)))
        args.append(arr)
        out_specs.append(pl.BlockSpec((rb, c), lambda i: (i, 0)))
        out_shape.append(jax.ShapeDtypeStruct((r, c), BF16))
    outs = pl.pallas_call(
        functools.partial(_ffn_body, final=final, n_cast=len(cast)),
        grid=(n_steps,),
        in_specs=in_specs,
        out_specs=out_specs,
        out_shape=out_shape,
        compiler_params=pltpu.CompilerParams(
            dimension_semantics=("arbitrary",), vmem_limit_bytes=VMEM_BIG),
        name="ffn_final" if final else "ffn",
    )(*args)
    return outs[0], outs[1:]


def _ret_body(x_ref, pos_ref, invf_ref, ng_ref, sh_ref, sc_ref, gt_ref,
              mask_ref, qd_ref, kd_ref, cd_ref, w_ref, wout_ref, o_ref,
              q_ref, k_ref, v_ref, g_ref, gated_ref, state_ref):
    @pl.when(pl.program_id(1) == 0)
    def _():
        state_ref[...] = jnp.zeros_like(state_ref)

    x = x_ref[...]
    h = _rms_mod(x, ng_ref[0], sh_ref[0], sc_ref[0]).astype(BF16)
    half = RET_QK_DIM // 2
    qk_w = RET_HEADS * RET_QK_DIM
    v_w = RET_HEADS * RET_V_DIM
    ang = pos_ref[...].astype(F32) * invf_ref[...]
    cos = jnp.cos(ang)
    sin = jnp.sin(ang)
    for hh in range(RET_HEADS):
        lo = hh * RET_V_DIM
        v_ref[:, lo:lo + RET_V_DIM] = jnp.dot(
            h, w_ref[:, 2 * qk_w + lo:2 * qk_w + lo + RET_V_DIM],
            preferred_element_type=F32).astype(BF16)
        g_ref[:, lo:lo + RET_V_DIM] = jnp.dot(
            h, w_ref[:, 2 * qk_w + v_w + lo:2 * qk_w + v_w + lo + RET_V_DIM],
            preferred_element_type=F32).astype(BF16)
    for base, dst, scale in ((0, q_ref, None), (qk_w, k_ref, RET_QK_DIM ** -0.5)):
        for hh in range(RET_HEADS):
            lo = hh * RET_QK_DIM
            t = jnp.dot(h, w_ref[:, base + lo:base + lo + RET_QK_DIM], preferred_element_type=F32)
            t1 = t[:, :half]
            t2 = t[:, half:]
            r1 = t1 * cos - t2 * sin
            r2 = t1 * sin + t2 * cos
            if scale is not None:
                r1 = r1 * scale
                r2 = r2 * scale
            dst[:, lo:lo + half] = r1.astype(BF16)
            dst[:, lo + half:lo + RET_QK_DIM] = r2.astype(BF16)

    for c in range(RET_TM // RET_CHUNK):
        rows = slice(c * RET_CHUNK, (c + 1) * RET_CHUNK)
        heads = range(RET_HEADS)
        qs = [q_ref[rows, hh * RET_QK_DIM:(hh + 1) * RET_QK_DIM] for hh in heads]
        ks = [k_ref[rows, hh * RET_QK_DIM:(hh + 1) * RET_QK_DIM] for hh in heads]
        vs = [v_ref[rows, hh * RET_V_DIM:(hh + 1) * RET_V_DIM] for hh in heads]
        scores = [lax.dot_general(qs[hh], ks[hh], (((1,), (1,)), ((), ())),
                                  preferred_element_type=F32) for hh in heads]
        updates = [lax.dot_general((ks[hh].astype(F32) * kd_ref[hh]).astype(BF16), vs[hh],
                                   (((0,), (0,)), ((), ())), preferred_element_type=F32)
                   for hh in heads]
        crosses = []
        for hh in heads:
            st = state_ref[hh]
            crosses.append(jnp.dot(qs[hh], st.astype(BF16), preferred_element_type=F32))
            state_ref[hh] = st * cd_ref[hh] + updates[hh]
        for hh in heads:
            gh = g_ref[rows, hh * RET_V_DIM:(hh + 1) * RET_V_DIM].astype(F32)
            inner = jnp.dot((scores[hh] * mask_ref[hh]).astype(BF16), vs[hh],
                            preferred_element_type=F32)
            o = inner + crosses[hh] * qd_ref[hh]
            mu = jnp.mean(o, axis=-1, keepdims=True)
            oc = o - mu
            var = jnp.mean(oc * oc, axis=-1, keepdims=True)
            on = oc * lax.rsqrt(var + GN_EPS)
            gated_ref[rows, hh * RET_V_DIM:(hh + 1) * RET_V_DIM] = (
                (gh * jax.nn.sigmoid(gh)) * on).astype(BF16)
    acc = jnp.dot(gated_ref[...], wout_ref[...], preferred_element_type=F32)
    o_ref[...] = x + gt_ref[0] * acc


def _ret(x2, pos2, inv_freq, ng3, mods3, norm_row, mod_row, gate_row, consts, w_in, w_out,
         bsz, tiles_per_batch):
    t, d = x2.shape
    qk_w = RET_HEADS * RET_QK_DIM
    v_w = RET_HEADS * RET_V_DIM
    mask, qd, kd, cd = consts
    tok = lambda w: pl.BlockSpec((RET_TM, w), lambda b, n: (b * tiles_per_batch + n, 0))
    full = lambda a: pl.BlockSpec(a.shape, lambda b, n: (0,) * a.ndim)
    return pl.pallas_call(
        _ret_body,
        grid=(bsz, tiles_per_batch),
        in_specs=[
            tok(d),
            tok(1),
            pl.BlockSpec((1, RET_QK_DIM // 2), lambda b, n: (0, 0)),
            _vec_spec(lambda b, n: norm_row),
            _vec_spec(lambda b, n: mod_row(b)),
            _vec_spec(lambda b, n: mod_row(b) + 1),
            _vec_spec(lambda b, n: gate_row(b)),
            full(mask), full(qd), full(kd), full(cd),
            _resident(w_in.shape),
            _resident(w_out.shape),
        ],
        out_specs=tok(d),
        out_shape=jax.ShapeDtypeStruct((t, d), F32),
        scratch_shapes=[pltpu.VMEM((RET_TM, qk_w), BF16), pltpu.VMEM((RET_TM, qk_w), BF16),
                        pltpu.VMEM((RET_TM, v_w), BF16), pltpu.VMEM((RET_TM, v_w), BF16),
                        pltpu.VMEM((RET_TM, v_w), BF16),
                        pltpu.VMEM((RET_HEADS, RET_QK_DIM, RET_V_DIM), F32)],
        compiler_params=pltpu.CompilerParams(
            dimension_semantics=("arbitrary", "arbitrary"), vmem_limit_bytes=VMEM_BIG),
        name="ret",
    )(x2, pos2, inv_freq, ng3, mods3, mods3, mods3, mask, qd, kd, cd, w_in, w_out)


def _ret_consts():
    hds, c = RET_HEADS, RET_CHUNK
    log_gamma = jnp.log(1.0 - jnp.power(2.0, -5.0 - jnp.arange(hds, dtype=F32)))
    pos = jnp.arange(c, dtype=F32)
    diff = pos[:, None] - pos[None, :]
    mask = jnp.where(diff >= 0, jnp.exp(log_gamma[:, None, None] * jnp.maximum(diff, 0.0)), 0.0)
    qd = jnp.exp(log_gamma[:, None] * (pos + 1.0))[:, :, None]
    kd = jnp.exp(log_gamma[:, None] * (c - 1.0 - pos))[:, :, None]
    cd = jnp.broadcast_to(jnp.exp(log_gamma * c)[:, None, None], (hds, 1, RET_V_DIM))
    return mask, qd, kd, cd


def _s5_pre_body(x_ref, ng_ref, sh_ref, sc_ref, o_ref):
    h = _rms_mod(x_ref[...], ng_ref[0], sh_ref[0], sc_ref[0])
    d = h.shape[1]
    for r in range(S5_TM // S5_CHUNK):
        ht = h[r * S5_CHUNK:(r + 1) * S5_CHUNK, :].T
        o_ref[:, r * SUBLANES:(r + 1) * SUBLANES, :] = ht.reshape(d // SUBLANES, SUBLANES, S5_CHUNK)


def _s5_pre(x2, ng3, mods3, norm_row, mod_row, tiles_per_batch):
    t, d = x2.shape
    rows = S5_TM // S5_CHUNK * SUBLANES
    return pl.pallas_call(
        _s5_pre_body,
        grid=(t // S5_TM,),
        in_specs=[pl.BlockSpec((S5_TM, d), lambda i: (i, 0)),
                  _vec_spec(lambda i: norm_row),
                  _vec_spec(lambda i: mod_row(i // tiles_per_batch)),
                  _vec_spec(lambda i: mod_row(i // tiles_per_batch) + 1)],
        out_specs=pl.BlockSpec((d // SUBLANES, rows, S5_CHUNK), lambda i: (0, i, 0)),
        out_shape=jax.ShapeDtypeStruct((d // SUBLANES, t // S5_CHUNK * SUBLANES, S5_CHUNK), F32),
        compiler_params=pltpu.CompilerParams(
            dimension_semantics=("arbitrary",), vmem_limit_bytes=VMEM_MID),
        name="s5_pre",
    )(x2, ng3, mods3, mods3)


def _pair(re, im):
    return jnp.concatenate([re, im], axis=1)


def _cmul2(x, xs, w1, w2):
    return x * w1 + xs * w2, xs * w1 - x * w2


def _s5_levels(chunks_per_seq):
    return S5_CHUNK.bit_length() - 1, (chunks_per_seq - 1).bit_length()


def _s5_terms(are_ref, aim_ref, ldt_ref, btr_ref, bti_ref, cr_ref, ci_ref, chunks_per_seq):
    pp = S5_STATE
    a_re = are_ref[0]
    a_im = aim_ref[0]
    dt = jnp.exp(ldt_ref[0])
    lam_re = dt * a_re
    lam_im = dt * a_im
    mag = jnp.exp(lam_re)
    n_tab, n_scan = _s5_levels(chunks_per_seq)
    n_lvl = n_tab + n_scan
    sq = [(mag * jnp.cos(lam_im), mag * jnp.sin(lam_im))]
    for _ in range(n_lvl - 1):
        sq.append(_cmul(sq[-1][0], sq[-1][1], sq[-1][0], sq[-1][1]))
    lvl_re = jnp.concatenate([r for r, _ in sq], axis=0)
    lvl_im = jnp.concatenate([i for _, i in sq], axis=0)
    w1_all = _pair(lvl_re, lvl_re)
    w2_all = _pair(-lvl_im, lvl_im)
    w1 = lambda b: w1_all[b:b + 1, :]
    w2 = lambda b: w2_all[b:b + 1, :]

    def power_table(first, ascending):
        lane = lax.broadcasted_iota(jnp.int32, (SUBLANES, 2 * pp), 1)
        x = jnp.where(lane < pp, 1.0, 0.0).astype(F32)
        xs = 1.0 - x
        for b in range(SUBLANES.bit_length()):
            nx, nxs = _cmul2(x, xs, w1(b), w2(b))
            bit = ((first >> b) & 1) == 1
            x = jnp.where(bit, nx, x)
            xs = jnp.where(bit, nxs, xs)
        for b in range(SUBLANES.bit_length() - 1, n_tab):
            nx, nxs = _cmul2(x, xs, w1(b), w2(b))
            x = jnp.concatenate([x, nx] if ascending else [nx, x], axis=0)
            xs = jnp.concatenate([xs, nxs] if ascending else [nxs, xs], axis=0)
        return x, xs

    nr, ni = sq[0][0] - 1.0, sq[0][1]
    den = a_re * a_re + a_im * a_im
    q_re = (nr * a_re + ni * a_im) / den
    q_im = (ni * a_re - nr * a_im) / den
    bbr, bbi = _cmul(q_re, q_im, btr_ref[0], bti_ref[0])
    bb = _pair(bbr, bbi)
    bbs = _pair(bbi, bbr)
    b1 = _pair(bbr, bbr)
    b2 = _pair(-bbi, bbi)
    c_re = cr_ref[0]
    c_im = ci_ref[0]
    c1 = _pair(c_re, -c_re)
    c2 = _pair(-c_im, -c_im)
    return dict(w1_all=w1_all, w2_all=w2_all, power_table=power_table,
                bb=bb, bbs=bbs, b1=b1, b2=b2, c1=c1, c2=c2)


def _s5_lags(*refs, chunks_per_seq):
    *param_refs, kt_ref = refs
    kk, pp = S5_GROUP, S5_STATE
    tm = _s5_terms(*param_refs, chunks_per_seq)
    sub = lax.broadcasted_iota(jnp.int32, (SUBLANES, 2 * pp), 0)
    e0, _ = tm["power_table"](sub, True)
    cb = jnp.concatenate([tm["bb"] * tm["c1"][k:k + 1, :] + tm["bbs"] * tm["c2"][k:k + 1, :]
                          for k in range(kk)], axis=0)
    kt_ref[...] = lax.dot_general(cb, e0, (((1,), (1,)), ((), ())),
                                  precision=lax.Precision.HIGHEST, preferred_element_type=F32)


def _s5_build(*refs, chunks_per_seq):
    *param_refs, kt_ref, toep_ref, ws_ref, wo_ref, w1_ref, w2_ref = refs
    kk, pp, tt = S5_GROUP, S5_STATE, S5_CHUNK
    tm = _s5_terms(*param_refs, chunks_per_seq)
    b1, b2, c1, c2 = tm["b1"], tm["b2"], tm["c1"], tm["c2"]
    sub = lax.broadcasted_iota(jnp.int32, (SUBLANES, 2 * pp), 0)
    e1, e1s = tm["power_table"](sub + 1, True)
    er, ers = tm["power_table"](SUBLANES - 1 - sub, False)

    causal = (lax.broadcasted_iota(jnp.int32, (tt, tt), 1)
              >= lax.broadcasted_iota(jnp.int32, (tt, tt), 0))
    for kp in range(kk):
        for k in range(kk):
            lag = kt_ref[k * kk + kp:k * kk + kp + 1, :]
            blk = pltpu.roll(jnp.broadcast_to(lag, (tt, tt)), 0, 1, stride=1, stride_axis=0)
            toep_ref[kp * tt:(kp + 1) * tt, k * tt:(k + 1) * tt] = (
                jnp.where(causal, blk, 0.0).astype(BF16))
    for kp in range(kk):
        ws_ref[kp * tt:(kp + 1) * tt, :] = (er * b1[kp:kp + 1, :] + ers * b2[kp:kp + 1, :]).astype(BF16)
    for k in range(kk):
        wo_ref[k * tt:(k + 1) * tt, :] = (e1 * c1[k:k + 1, :] + e1s * c2[k:k + 1, :]).astype(BF16)
    w1_ref[...] = tm["w1_all"]
    w2_ref[...] = tm["w2_all"]


def _s5_apply(u_refs, d_ref, o_refs, toep_ref, ws_ref, wo_ref, w1_ref, w2_ref, ucat_ref, *,
              chunks_per_seq):
    kk, pp, tt = S5_GROUP, S5_STATE, S5_CHUNK
    n_tab, n_scan = _s5_levels(chunks_per_seq)
    n_rows = u_refs[0].shape[0] // SUBLANES

    def chan(block_refs, k):
        return block_refs[k // SUBLANES], pl.ds(k % SUBLANES, n_rows, stride=SUBLANES)

    us = []
    for k in range(kk):
        flat, rows_k = chan(u_refs, k)
        us.append(flat[rows_k, :])
    for kp in range(kk):
        ucat_ref[:, kp * tt:(kp + 1) * tt] = us[kp].astype(BF16)
    ucat = ucat_ref[...]

    s = jnp.dot(ucat, ws_ref[...], preferred_element_type=F32)
    ss = pltpu.roll(s, pp, 1)
    cidx = lax.broadcasted_iota(jnp.int32, (n_rows, 2 * pp), 0) % chunks_per_seq
    for lvl in range(n_scan):
        sh = 1 << lvl
        row = slice(n_tab + lvl, n_tab + lvl + 1)
        tx, txs = _cmul2(pltpu.roll(s, sh, 0), pltpu.roll(ss, sh, 0), w1_ref[row, :], w2_ref[row, :])
        keep = cidx >= sh
        s = s + jnp.where(keep, tx, 0.0)
        ss = ss + jnp.where(keep, txs, 0.0)
    prev = jnp.where(cidx >= 1, pltpu.roll(s, 1, 0), 0.0).astype(BF16)

    y = jnp.dot(ucat, toep_ref[...], preferred_element_type=F32)
    y = y + lax.dot_general(prev, wo_ref[...], (((1,), (1,)), ((), ())),
                            preferred_element_type=F32)
    for k in range(kk):
        z = y[:, k * tt:(k + 1) * tt] + d_ref[0, k:k + 1, :] * us[k]
        flat, rows_k = chan(o_refs, k)
        flat[rows_k, :] = jax.nn.gelu(z)


def _s5_core_body(*refs, chunks_per_seq):
    n_half = S5_GROUP // SUBLANES
    n_par = 7
    u_refs = refs[:n_half]
    cur_refs = refs[n_half:n_half + n_par]
    nxt_refs = refs[n_half + n_par:n_half + 2 * n_par]
    d_ref = refs[n_half + 2 * n_par]
    rest = refs[n_half + 2 * n_par + 1:]
    o_refs, rest = rest[:n_half], rest[n_half:]
    slots = (rest[0:6], rest[6:12])
    ucat_ref = rest[12]
    step = pl.program_id(0)

    @pl.when(step == 0)
    def _():
        for ref in slots[1][1:]:
            ref[...] = jnp.zeros_like(ref)
        _s5_lags(*cur_refs, slots[0][0], chunks_per_seq=chunks_per_seq)

    for parity in range(2):
        @pl.when(step % 2 == parity)
        def _():
            _s5_build(*cur_refs, *slots[parity], chunks_per_seq=chunks_per_seq)
            _s5_apply(u_refs, d_ref, o_refs, *slots[1 - parity][1:], ucat_ref,
                      chunks_per_seq=chunks_per_seq)
            _s5_lags(*nxt_refs, slots[1 - parity][0], chunks_per_seq=chunks_per_seq)


def _s5_core(ht, a_re, a_im, log_dt, bt_re, bt_im, c_re, c_im, d_b, chunks_per_seq):
    _, flat_rows, tt = ht.shape
    rows = flat_rows // SUBLANES
    g, kk, pp = S5_GROUPS, S5_GROUP, S5_STATE
    n_lvl = sum(_s5_levels(chunks_per_seq))
    applied = lambda s: jnp.maximum(s - 1, 0)

    def params(ahead):
        group = lambda s: jnp.minimum(s + ahead, g - 1)
        par = lambda *shape: pl.BlockSpec((1,) + shape, lambda s: (group(s),) + (0,) * len(shape))
        return [par(1, pp), par(1, pp), par(1, 1), par(kk, pp), par(kk, pp), par(kk, pp), par(kk, pp)]

    n_half = kk // SUBLANES
    halves = [pl.BlockSpec((None, flat_rows, tt), lambda s, h=h: (applied(s) * n_half + h, 0, 0))
              for h in range(n_half)]
    slot = [pltpu.VMEM((kk * kk, tt), F32),
            pltpu.VMEM((kk * tt, kk * tt), BF16),
            pltpu.VMEM((kk * tt, 2 * pp), BF16), pltpu.VMEM((kk * tt, 2 * pp), BF16),
            pltpu.VMEM((n_lvl, 2 * pp), F32), pltpu.VMEM((n_lvl, 2 * pp), F32)]
    prm = (a_re, a_im, log_dt, bt_re, bt_im, c_re, c_im)
    return pl.pallas_call(
        functools.partial(_s5_core_body, chunks_per_seq=chunks_per_seq),
        grid=(g + 1,),
        in_specs=halves + params(0) + params(1) + [
            pl.BlockSpec((1, kk, tt), lambda s: (applied(s), 0, 0))],
        out_specs=[pl.BlockSpec((None, flat_rows, tt), lambda s: (applied(s), 0, 0))] * n_half,
        out_shape=[jax.ShapeDtypeStruct((g, flat_rows, tt), F32)] * n_half,
        scratch_shapes=slot + slot + [pltpu.VMEM((rows, kk * tt), BF16)],
        compiler_params=pltpu.CompilerParams(
            dimension_semantics=("arbitrary",), vmem_limit_bytes=VMEM_BIG),
        name="s5_core",
    )(*([ht] * n_half), *prm, *prm, d_b)


def _s5_post_body(*refs):
    n_half = S5_GROUP // SUBLANES
    y_refs = refs[:n_half]
    x_ref, gt_ref, w_ref, o_ref, ys_ref = refs[n_half:]
    for r in range(S5_TM // S5_CHUNK):
        yt = jnp.concatenate(
            [y_refs[h][:, r * SUBLANES:(r + 1) * SUBLANES, :].reshape(S5_GROUPS * SUBLANES, S5_CHUNK)
             for h in range(n_half)], axis=0)
        ys_ref[r * S5_CHUNK:(r + 1) * S5_CHUNK, :] = yt.T.astype(BF16)
    yb = ys_ref[...]
    d = x_ref.shape[1]
    y1 = jnp.dot(yb, w_ref[:, :d], preferred_element_type=F32)
    y2 = jnp.dot(yb, w_ref[:, d:], preferred_element_type=F32)
    o_ref[...] = x_ref[...] + gt_ref[0] * (y1 * jax.nn.sigmoid(y2))


def _s5_post(yts, x2, mods3, gate_row, w_glu, tiles_per_batch):
    t, d = x2.shape
    rows = S5_TM // S5_CHUNK
    return pl.pallas_call(
        _s5_post_body,
        grid=(t // S5_TM,),
        in_specs=[pl.BlockSpec((S5_GROUPS, rows * SUBLANES, S5_CHUNK),
                               lambda i: (0, i, 0))] * (S5_GROUP // SUBLANES) + [
                  pl.BlockSpec((S5_TM, d), lambda i: (i, 0)),
                  _vec_spec(lambda i: gate_row(i // tiles_per_batch)),
                  _resident(w_glu.shape)],
        out_specs=pl.BlockSpec((S5_TM, d), lambda i: (i, 0)),
        out_shape=jax.ShapeDtypeStruct((t, d), F32),
        scratch_shapes=[pltpu.VMEM((S5_TM, d), BF16)],
        compiler_params=pltpu.CompilerParams(
            dimension_semantics=("arbitrary",), vmem_limit_bytes=VMEM_BIG),
        name="s5_post",
    )(*yts, x2, mods3, w_glu)


def kernel(x, c, positions, ada_w, ada_b, norm_g, ffn_w_in, ffn_w_out, ret_w_in, ret_w_out,
           s5_a_re, s5_a_im, s5_b_re, s5_b_im, s5_c_re, s5_c_im, s5_d, s5_log_dt, s5_w_glu,
           final_g):
    bsz, seq, d = x.shape
    depth = ada_w.shape[0]
    assert d == D_MODEL and seq % S5_TM == 0 and seq % RET_TM == 0 and RET_TM % RET_CHUNK == 0
    t = bsz * seq
    g, kk, pp = S5_GROUPS, S5_GROUP, S5_STATE

    mods3 = _ada(c, ada_w, ada_b).reshape(depth * bsz * N_MOD, 1, d)
    ng3 = norm_g.reshape(depth * 3, 1, d)
    x2 = x.reshape(t, d)
    pos2 = positions.reshape(t, 1)
    half = RET_QK_DIM // 2
    inv_freq = (ROPE_THETA ** (-jnp.arange(half, dtype=F32) / half)).reshape(1, half)
    ret_consts = _ret_consts()

    w_ffn = (ffn_w_in[0, 0].astype(BF16), ffn_w_out[0, 0].astype(BF16))
    for i in range(depth):
        mod_row = lambda s: (lambda b: (i * bsz + b) * N_MOD + 3 * s)
        gate_row = lambda s: (lambda b: (i * bsz + b) * N_MOD + 3 * s + 2)
        j = i // 2
        cast = [(ffn_w_in, (i, 1)), (ffn_w_out, (i, 1))]
        if i % 2 == 1:
            w_glu = s5_w_glu[j].reshape(g, kk // SUBLANES, SUBLANES, -1).transpose(1, 0, 2, 3)
            cast.append((w_glu.reshape(d, -1), ()))
        x2, done = _ffn(x2, ng3, mods3, i * 3, mod_row(0), *w_ffn, seq // FFN_TM, cast=cast)
        w_ffn = done[:2]
        if i % 2 == 0:
            x2 = _ret(x2, pos2, inv_freq, ng3, mods3, i * 3 + 1, mod_row(1), gate_row(1),
                      ret_consts, ret_w_in[j].astype(BF16), ret_w_out[j].astype(BF16),
                      bsz, seq // RET_TM)
        else:
            ht = _s5_pre(x2, ng3, mods3, i * 3 + 1, mod_row(1), seq // S5_TM)
            yts = _s5_core(
                ht,
                s5_a_re[j].reshape(g, 1, pp), s5_a_im[j].reshape(g, 1, pp),
                s5_log_dt[j].reshape(g, 1, 1),
                s5_b_re[j].transpose(0, 2, 1), s5_b_im[j].transpose(0, 2, 1),
                s5_c_re[j], s5_c_im[j],
                jnp.broadcast_to(s5_d[j].reshape(g, kk, 1), (g, kk, S5_CHUNK)),
                seq // S5_CHUNK)
            x2 = _s5_post(yts, x2, mods3, gate_row(1), done[2], seq // S5_TM)
        last = i == depth - 1
        cast = [] if last else [(ffn_w_in, (i + 1, 0)), (ffn_w_out, (i + 1, 0))]
        x2, done = _ffn(x2, ng3, mods3, i * 3 + 2, mod_row(2), *w_ffn, seq // FFN_TM,
                        final_g=final_g if last else None, cast=cast)
        w_ffn = done[:2]
    return x2.reshape(bsz, seq, d)
```

```python
import functools

import jax
import jax.numpy as jnp
from jax import lax
from jax.experimental import pallas as pl
from jax.experimental.pallas import tpu as pltpu

F32 = jnp.float32
BF16 = jnp.bfloat16

D_MODEL = 1024
N_MOD = 9
D_FF = 2816
RET_HEADS = 4
RET_QK_DIM = 256
RET_V_DIM = 512
RET_CHUNK = 256
ROPE_THETA = 10000.0
S5_GROUP = 16
S5_GROUPS = 64
S5_STATE = 64
NORM_EPS = 1e-6
GN_EPS = 1e-5

LANES = 128
SUBLANES = 8
MXU_DIM = 256
S5_CHUNK = LANES
VMEM_BIG = 56 * 1024 * 1024
VMEM_MID = 40 * 1024 * 1024

FFN_TM = 1024
FFN_CHUNKS = ((0, 4 * MXU_DIM), (4 * MXU_DIM, 8 * MXU_DIM), (8 * MXU_DIM, D_FF))
assert D_FF % MXU_DIM == 0
RET_TM = 512
ADA_TN = 2304
S5_TM = 1024


def _rms_mod(x, g, shift, scale):
    y = x * lax.rsqrt(jnp.mean(x * x, axis=-1, keepdims=True) + NORM_EPS)
    return (y * g) * (1.0 + scale) + shift


def _cmul(ar, ai, br, bi):
    return ar * br - ai * bi, ar * bi + ai * br


def _resident(shape):
    zeros = (0,) * len(shape)
    return pl.BlockSpec(shape, lambda *_: zeros, pipeline_mode=pl.Buffered(1))


def _ada_body(ct_ref, w_ref, b_ref, o_ref):
    ct = ct_ref[...]
    ca = ct * jax.nn.sigmoid(ct)
    w = w_ref[0]
    for b in range(ct.shape[1]):
        o_ref[0, b:b + 1, :] = jnp.sum(w * ca[:, b:b + 1], axis=0, keepdims=True) + b_ref[0]


def _ada(c, ada_w, ada_b):
    depth, d, n = ada_w.shape
    bsz = c.shape[0]
    return pl.pallas_call(
        _ada_body,
        grid=(depth, n // ADA_TN),
        in_specs=[
            pl.BlockSpec((d, bsz), lambda i, j: (0, 0)),
            pl.BlockSpec((1, d, ADA_TN), lambda i, j: (i, 0, j)),
            pl.BlockSpec((1, 1, ADA_TN), lambda i, j: (i, 0, j)),
        ],
        out_specs=pl.BlockSpec((1, bsz, ADA_TN), lambda i, j: (i, 0, j)),
        out_shape=jax.ShapeDtypeStruct((depth, bsz, n), F32),
        compiler_params=pltpu.CompilerParams(
            dimension_semantics=("arbitrary", "arbitrary"), vmem_limit_bytes=VMEM_MID),
        name="ada",
    )(c.T, ada_w, ada_b.reshape(depth, 1, n))


def _ffn_body(x_ref, ng_ref, sh_ref, sc_ref, gt_ref, win_ref, wout_ref, *rest, final, n_cast):
    if final:
        fg_ref, rest = rest[0], rest[1:]
    src_refs, o_ref, dst_refs = rest[:n_cast], rest[n_cast], rest[n_cast + 1:]
    for src, dst in zip(src_refs, dst_refs):
        dst[...] = src[...].astype(BF16)
    x = x_ref[...]
    h = _rms_mod(x, ng_ref[0], sh_ref[0], sc_ref[0]).astype(BF16)
    acc = None
    for lo, hi in FFN_CHUNKS:
        g = jnp.dot(h, win_ref[:, lo:hi], preferred_element_type=F32)
        u = jnp.dot(h, win_ref[:, D_FF + lo:D_FF + hi], preferred_element_type=F32)
        a = (g * jax.nn.sigmoid(g) * u).astype(BF16)
        p = jnp.dot(a, wout_ref[lo:hi, :], preferred_element_type=F32)
        acc = p if acc is None else acc + p
    y = x + (0.5 * gt_ref[0]) * acc
    if final:
        y = (y * lax.rsqrt(jnp.mean(y * y, axis=-1, keepdims=True) + NORM_EPS)) * fg_ref[...]
    o_ref[...] = y


def _vec_spec(row_fn):
    return pl.BlockSpec((1, 1, D_MODEL), lambda *idx: (row_fn(*idx), 0, 0))


def _ffn(x2, ng3, mods3, norm_row, mod_row, w_in, w_out, tiles_per_batch, final_g=None, cast=()):
    t, d = x2.shape
    n_steps = t // FFN_TM
    final = final_g is not None
    in_specs = [
        pl.BlockSpec((FFN_TM, d), lambda i: (i, 0)),
        _vec_spec(lambda i: norm_row),
        _vec_spec(lambda i: mod_row(i // tiles_per_batch)),
        _vec_spec(lambda i: mod_row(i // tiles_per_batch) + 1),
        _vec_spec(lambda i: mod_row(i // tiles_per_batch) + 2),
        _resident(w_in.shape),
        _resident(w_out.shape),
    ]
    args = [x2, ng3, mods3, mods3, mods3, w_in, w_out]
    if final:
        in_specs.append(pl.BlockSpec((1, d), lambda i: (0, 0)))
        args.append(final_g.reshape(1, d))
    out_specs = [pl.BlockSpec((FFN_TM, d), lambda i: (i, 0))]
    out_shape = [jax.ShapeDtypeStruct((t, d), F32)]
    for arr, lead in cast:
        r, c = arr.shape[len(lead):]
        rb = r // n_steps
        assert rb * n_steps == r and rb % (2 * SUBLANES) == 0, (r, n_steps)
        in_specs.append(pl.BlockSpec((None,) * len(lead) + (rb, c),
                                     lambda i, lead=lead: tuple(lead) + (i, 0)))
        args.append(arr)
        out_specs.append(pl.BlockSpec((rb, c), lambda i: (i, 0)))
        out_shape.append(jax.ShapeDtypeStruct((r, c), BF16))
    outs = pl.pallas_call(
        functools.partial(_ffn_body, final=final, n_cast=len(cast)),
        grid=(n_steps,),
        in_specs=in_specs,
        out_specs=out_specs,
        out_shape=out_shape,
        compiler_params=pltpu.CompilerParams(
            dimension_semantics=("arbitrary",), vmem_limit_bytes=VMEM_BIG),
        name="ffn_final" if final else "ffn",
    )(*args)
    return outs[0], outs[1:]


def _ret_body(x_ref, pos_ref, invf_ref, ng_ref, sh_ref, sc_ref, gt_ref,
              mask_ref, qd_ref, kd_ref, cd_ref, w_ref, wout_ref, o_ref,
              q_ref, qx_ref, k_ref, kx_ref, v_ref, g_ref, gated_ref, state_ref):
    @pl.when(pl.program_id(1) == 0)
    def _():
        state_ref[...] = jnp.zeros_like(state_ref)

    x = x_ref[...]
    h = _rms_mod(x, ng_ref[0], sh_ref[0], sc_ref[0]).astype(BF16)
    half = RET_QK_DIM // 2
    qk_w = RET_HEADS * RET_QK_DIM
    v_w = RET_HEADS * RET_V_DIM
    ang = pos_ref[...].astype(F32) * invf_ref[...]
    cos = jnp.cos(ang)
    sin = jnp.sin(ang)
    for hh in range(RET_HEADS):
        qk = slice(hh * RET_QK_DIM, (hh + 1) * RET_QK_DIM)
        vv = slice(hh * RET_V_DIM, (hh + 1) * RET_V_DIM)
        v_ref[:, vv] = jnp.dot(h, w_ref[:, 2 * qk_w + vv.start:2 * qk_w + vv.stop],
                               preferred_element_type=F32).astype(BF16)
        gate = jnp.dot(h, w_ref[:, 2 * qk_w + v_w + vv.start:2 * qk_w + v_w + vv.stop],
                       preferred_element_type=F32)
        g_ref[:, vv] = (gate * jax.nn.sigmoid(gate)).astype(BF16)
        for base, dst, dstx, dec_ref, scale in ((0, q_ref, qx_ref, qd_ref, None),
                                                (qk_w, k_ref, kx_ref, kd_ref, RET_QK_DIM ** -0.5)):
            t = jnp.dot(h, w_ref[:, base + qk.start:base + qk.stop], preferred_element_type=F32)
            t1 = t[:, :half]
            t2 = t[:, half:]
            r1 = t1 * cos - t2 * sin
            r2 = t1 * sin + t2 * cos
            if scale is not None:
                r1 = r1 * scale
                r2 = r2 * scale
            dst[:, qk.start:qk.start + half] = r1.astype(BF16)
            dst[:, qk.start + half:qk.stop] = r2.astype(BF16)
            dec = dec_ref[hh]
            dstx[:, qk.start:qk.start + half] = (r1 * dec).astype(BF16)
            dstx[:, qk.start + half:qk.stop] = (r2 * dec).astype(BF16)

        for c in range(RET_TM // RET_CHUNK):
            rows = slice(c * RET_CHUNK, (c + 1) * RET_CHUNK)
            vh = v_ref[rows, vv]
            st = state_ref[hh]
            scores = lax.dot_general(q_ref[rows, qk], k_ref[rows, qk], (((1,), (1,)), ((), ())),
                                     preferred_element_type=F32)
            update = lax.dot_general(kx_ref[rows, qk], vh, (((0,), (0,)), ((), ())),
                                     preferred_element_type=F32)
            cross = jnp.dot(qx_ref[rows, qk], st.astype(BF16), preferred_element_type=F32)
            state_ref[hh] = st * cd_ref[hh] + update
            inner = jnp.dot((scores * mask_ref[hh]).astype(BF16), vh, preferred_element_type=F32)
            o = inner + cross
            mu = jnp.mean(o, axis=-1, keepdims=True)
            oc = o - mu
            var = jnp.mean(oc * oc, axis=-1, keepdims=True)
            on = oc * lax.rsqrt(var + GN_EPS)
            gated_ref[rows, vv] = g_ref[rows, vv] * on.astype(BF16)
    acc = jnp.dot(gated_ref[...], wout_ref[...], preferred_element_type=F32)
    o_ref[...] = x + gt_ref[0] * acc


def _ret(x2, pos2, inv_freq, ng3, mods3, norm_row, mod_row, gate_row, consts, w_in, w_out,
         bsz, tiles_per_batch):
    t, d = x2.shape
    qk_w = RET_HEADS * RET_QK_DIM
    v_w = RET_HEADS * RET_V_DIM
    mask, qd, kd, cd = consts
    tok = lambda w: pl.BlockSpec((RET_TM, w), lambda b, n: (b * tiles_per_batch + n, 0))
    full = lambda a: pl.BlockSpec(a.shape, lambda b, n: (0,) * a.ndim)
    return pl.pallas_call(
        _ret_body,
        grid=(bsz, tiles_per_batch),
        in_specs=[
            tok(d),
            tok(1),
            pl.BlockSpec((1, RET_QK_DIM // 2), lambda b, n: (0, 0)),
            _vec_spec(lambda b, n: norm_row),
            _vec_spec(lambda b, n: mod_row(b)),
            _vec_spec(lambda b, n: mod_row(b) + 1),
            _vec_spec(lambda b, n: gate_row(b)),
            full(mask), full(qd), full(kd), full(cd),
            _resident(w_in.shape),
            _resident(w_out.shape),
        ],
        out_specs=tok(d),
        out_shape=jax.ShapeDtypeStruct((t, d), F32),
        scratch_shapes=[pltpu.VMEM((RET_TM, qk_w), BF16), pltpu.VMEM((RET_TM, qk_w), BF16),
                        pltpu.VMEM((RET_TM, qk_w), BF16), pltpu.VMEM((RET_TM, qk_w), BF16),
                        pltpu.VMEM((RET_TM, v_w), BF16), pltpu.VMEM((RET_TM, v_w), BF16),
                        pltpu.VMEM((RET_TM, v_w), BF16),
                        pltpu.VMEM((RET_HEADS, RET_QK_DIM, RET_V_DIM), F32)],
        compiler_params=pltpu.CompilerParams(
            dimension_semantics=("arbitrary", "arbitrary"), vmem_limit_bytes=VMEM_BIG),
        name="ret",
    )(x2, pos2, inv_freq, ng3, mods3, mods3, mods3, mask, qd, kd, cd, w_in, w_out)


def _ret_consts():
    hds, c = RET_HEADS, RET_CHUNK
    log_gamma = jnp.log(1.0 - jnp.power(2.0, -5.0 - jnp.arange(hds, dtype=F32)))
    pos = jnp.arange(c, dtype=F32)
    diff = pos[:, None] - pos[None, :]
    mask = jnp.where(diff >= 0, jnp.exp(log_gamma[:, None, None] * jnp.maximum(diff, 0.0)), 0.0)
    reps = (1, RET_TM // c, 1)
    qd = jnp.tile(jnp.exp(log_gamma[:, None] * (pos + 1.0))[:, :, None], reps)
    kd = jnp.tile(jnp.exp(log_gamma[:, None] * (c - 1.0 - pos))[:, :, None], reps)
    cd = jnp.broadcast_to(jnp.exp(log_gamma * c)[:, None, None], (hds, 1, RET_V_DIM))
    return mask, qd, kd, cd


def _s5_pre_body(x_ref, ng_ref, sh_ref, sc_ref, o_ref):
    h = _rms_mod(x_ref[...], ng_ref[0], sh_ref[0], sc_ref[0])
    d = h.shape[1]
    for r in range(S5_TM // S5_CHUNK):
        ht = h[r * S5_CHUNK:(r + 1) * S5_CHUNK, :].T
        o_ref[:, r * SUBLANES:(r + 1) * SUBLANES, :] = ht.reshape(d // SUBLANES, SUBLANES, S5_CHUNK)


def _s5_pre(x2, ng3, mods3, norm_row, mod_row, tiles_per_batch):
    t, d = x2.shape
    rows = S5_TM // S5_CHUNK * SUBLANES
    return pl.pallas_call(
        _s5_pre_body,
        grid=(t // S5_TM,),
        in_specs=[pl.BlockSpec((S5_TM, d), lambda i: (i, 0)),
                  _vec_spec(lambda i: norm_row),
                  _vec_spec(lambda i: mod_row(i // tiles_per_batch)),
                  _vec_spec(lambda i: mod_row(i // tiles_per_batch) + 1)],
        out_specs=pl.BlockSpec((d // SUBLANES, rows, S5_CHUNK), lambda i: (0, i, 0)),
        out_shape=jax.ShapeDtypeStruct((d // SUBLANES, t // S5_CHUNK * SUBLANES, S5_CHUNK), F32),
        compiler_params=pltpu.CompilerParams(
            dimension_semantics=("arbitrary",), vmem_limit_bytes=VMEM_MID),
        name="s5_pre",
    )(x2, ng3, mods3, mods3)


def _pair(re, im):
    return jnp.concatenate([re, im], axis=1)


def _cmul2(x, xs, w1, w2):
    return x * w1 + xs * w2, xs * w1 - x * w2


def _s5_levels(chunks_per_seq):
    return S5_CHUNK.bit_length() - 1, (chunks_per_seq - 1).bit_length()


def _s5_terms(are_ref, aim_ref, ldt_ref, btr_ref, bti_ref, cr_ref, ci_ref, chunks_per_seq):
    pp = S5_STATE
    a_re = are_ref[0]
    a_im = aim_ref[0]
    dt = jnp.exp(ldt_ref[0])
    lam_re = dt * a_re
    lam_im = dt * a_im
    mag = jnp.exp(lam_re)
    n_tab, n_scan = _s5_levels(chunks_per_seq)
    n_lvl = n_tab + n_scan
    sq = [(mag * jnp.cos(lam_im), mag * jnp.sin(lam_im))]
    for _ in range(n_lvl - 1):
        sq.append(_cmul(sq[-1][0], sq[-1][1], sq[-1][0], sq[-1][1]))
    lvl_re = jnp.concatenate([r for r, _ in sq], axis=0)
    lvl_im = jnp.concatenate([i for _, i in sq], axis=0)
    w1_all = _pair(lvl_re, lvl_re)
    w2_all = _pair(-lvl_im, lvl_im)
    w1 = lambda b: w1_all[b:b + 1, :]
    w2 = lambda b: w2_all[b:b + 1, :]

    def power_table(first, ascending):
        lane = lax.broadcasted_iota(jnp.int32, (SUBLANES, 2 * pp), 1)
        x = jnp.where(lane < pp, 1.0, 0.0).astype(F32)
        xs = 1.0 - x
        for b in range(SUBLANES.bit_length()):
            nx, nxs = _cmul2(x, xs, w1(b), w2(b))
            bit = ((first >> b) & 1) == 1
            x = jnp.where(bit, nx, x)
            xs = jnp.where(bit, nxs, xs)
        for b in range(SUBLANES.bit_length() - 1, n_tab):
            nx, nxs = _cmul2(x, xs, w1(b), w2(b))
            x = jnp.concatenate([x, nx] if ascending else [nx, x], axis=0)
            xs = jnp.concatenate([xs, nxs] if ascending else [nxs, xs], axis=0)
        return x, xs

    nr, ni = sq[0][0] - 1.0, sq[0][1]
    den = a_re * a_re + a_im * a_im
    q_re = (nr * a_re + ni * a_im) / den
    q_im = (ni * a_re - nr * a_im) / den
    bbr, bbi = _cmul(q_re, q_im, btr_ref[0], bti_ref[0])
    bb = _pair(bbr, bbi)
    bbs = _pair(bbi, bbr)
    b1 = _pair(bbr, bbr)
    b2 = _pair(-bbi, bbi)
    c_re = cr_ref[0]
    c_im = ci_ref[0]
    c1 = _pair(c_re, -c_re)
    c2 = _pair(-c_im, -c_im)
    return dict(w1_all=w1_all, w2_all=w2_all, power_table=power_table,
                bb=bb, bbs=bbs, b1=b1, b2=b2, c1=c1, c2=c2)


def _s5_lags(*refs, chunks_per_seq):
    *param_refs, kt_ref = refs
    kk, pp = S5_GROUP, S5_STATE
    tm = _s5_terms(*param_refs, chunks_per_seq)
    sub = lax.broadcasted_iota(jnp.int32, (SUBLANES, 2 * pp), 0)
    e0, _ = tm["power_table"](sub, True)
    cb = jnp.concatenate([tm["bb"] * tm["c1"][k:k + 1, :] + tm["bbs"] * tm["c2"][k:k + 1, :]
                          for k in range(kk)], axis=0)
    kt_ref[...] = lax.dot_general(cb, e0, (((1,), (1,)), ((), ())),
                                  precision=lax.Precision.HIGHEST, preferred_element_type=F32)


def _s5_build(*refs, chunks_per_seq):
    *param_refs, kt_ref, toep_ref, ws_ref, wo_ref, w1_ref, w2_ref = refs
    kk, pp, tt = S5_GROUP, S5_STATE, S5_CHUNK
    tm = _s5_terms(*param_refs, chunks_per_seq)
    b1, b2, c1, c2 = tm["b1"], tm["b2"], tm["c1"], tm["c2"]
    sub = lax.broadcasted_iota(jnp.int32, (SUBLANES, 2 * pp), 0)
    e1, e1s = tm["power_table"](sub + 1, True)
    er, ers = tm["power_table"](SUBLANES - 1 - sub, False)

    causal = (lax.broadcasted_iota(jnp.int32, (tt, tt), 1)
              >= lax.broadcasted_iota(jnp.int32, (tt, tt), 0))
    for kp in range(kk):
        for k in range(kk):
            lag = kt_ref[k * kk + kp:k * kk + kp + 1, :]
            blk = pltpu.roll(jnp.broadcast_to(lag, (tt, tt)), 0, 1, stride=1, stride_axis=0)
            toep_ref[kp * tt:(kp + 1) * tt, k * tt:(k + 1) * tt] = (
                jnp.where(causal, blk, 0.0).astype(BF16))
    for kp in range(kk):
        ws_ref[kp * tt:(kp + 1) * tt, :] = (er * b1[kp:kp + 1, :] + ers * b2[kp:kp + 1, :]).astype(BF16)
    for k in range(kk):
        wo_ref[k * tt:(k + 1) * tt, :] = (e1 * c1[k:k + 1, :] + e1s * c2[k:k + 1, :]).astype(BF16)
    w1_ref[...] = tm["w1_all"]
    w2_ref[...] = tm["w2_all"]


def _s5_apply(u_refs, d_ref, o_refs, toep_ref, ws_ref, wo_ref, w1_ref, w2_ref, ucat_ref, *,
              chunks_per_seq):
    kk, pp, tt = S5_GROUP, S5_STATE, S5_CHUNK
    n_tab, n_scan = _s5_levels(chunks_per_seq)
    n_rows = u_refs[0].shape[0] // SUBLANES

    def chan(block_refs, k):
        return block_refs[k // SUBLANES], pl.ds(k % SUBLANES, n_rows, stride=SUBLANES)

    us = []
    for k in range(kk):
        flat, rows_k = chan(u_refs, k)
        us.append(flat[rows_k, :])
    for kp in range(kk):
        ucat_ref[:, kp * tt:(kp + 1) * tt] = us[kp].astype(BF16)
    ucat = ucat_ref[...]

    s = jnp.dot(ucat, ws_ref[...], preferred_element_type=F32)
    ss = pltpu.roll(s, pp, 1)
    cidx = lax.broadcasted_iota(jnp.int32, (n_rows, 2 * pp), 0) % chunks_per_seq
    for lvl in range(n_scan):
        sh = 1 << lvl
        row = slice(n_tab + lvl, n_tab + lvl + 1)
        tx, txs = _cmul2(pltpu.roll(s, sh, 0), pltpu.roll(ss, sh, 0), w1_ref[row, :], w2_ref[row, :])
        keep = cidx >= sh
        s = s + jnp.where(keep, tx, 0.0)
        ss = ss + jnp.where(keep, txs, 0.0)
    prev = jnp.where(cidx >= 1, pltpu.roll(s, 1, 0), 0.0).astype(BF16)

    y = jnp.dot(ucat, toep_ref[...], preferred_element_type=F32)
    y = y + lax.dot_general(prev, wo_ref[...], (((1,), (1,)), ((), ())),
                            preferred_element_type=F32)
    for k in range(kk):
        z = y[:, k * tt:(k + 1) * tt] + d_ref[0, k:k + 1, :] * us[k]
        flat, rows_k = chan(o_refs, k)
        flat[rows_k, :] = jax.nn.gelu(z)


def _s5_core_body(*refs, chunks_per_seq):
    n_half = S5_GROUP // SUBLANES
    n_par = 7
    u_refs = refs[:n_half]
    cur_refs = refs[n_half:n_half + n_par]
    nxt_refs = refs[n_half + n_par:n_half + 2 * n_par]
    d_ref = refs[n_half + 2 * n_par]
    rest = refs[n_half + 2 * n_par + 1:]
    o_refs, rest = rest[:n_half], rest[n_half:]
    slots = (rest[0:6], rest[6:12])
    ucat_ref = rest[12]
    step = pl.program_id(0)

    @pl.when(step == 0)
    def _():
        for ref in slots[1][1:]:
            ref[...] = jnp.zeros_like(ref)
        _s5_lags(*cur_refs, slots[0][0], chunks_per_seq=chunks_per_seq)

    for parity in range(2):
        @pl.when(step % 2 == parity)
        def _():
            _s5_build(*cur_refs, *slots[parity], chunks_per_seq=chunks_per_seq)
            _s5_apply(u_refs, d_ref, o_refs, *slots[1 - parity][1:], ucat_ref,
                      chunks_per_seq=chunks_per_seq)
            _s5_lags(*nxt_refs, slots[1 - parity][0], chunks_per_seq=chunks_per_seq)


def _s5_core(ht, a_re, a_im, log_dt, bt_re, bt_im, c_re, c_im, d_b, chunks_per_seq):
    _, flat_rows, tt = ht.shape
    rows = flat_rows // SUBLANES
    g, kk, pp = S5_GROUPS, S5_GROUP, S5_STATE
    n_lvl = sum(_s5_levels(chunks_per_seq))
    applied = lambda s: jnp.maximum(s - 1, 0)

    def params(ahead):
        group = lambda s: jnp.minimum(s + ahead, g - 1)
        par = lambda *shape: pl.BlockSpec((1,) + shape, lambda s: (group(s),) + (0,) * len(shape))
        return [par(1, pp), par(1, pp), par(1, 1), par(kk, pp), par(kk, pp), par(kk, pp), par(kk, pp)]

    n_half = kk // SUBLANES
    halves = [pl.BlockSpec((None, flat_rows, tt), lambda s, h=h: (applied(s) * n_half + h, 0, 0))
              for h in range(n_half)]
    slot = [pltpu.VMEM((kk * kk, tt), F32),
            pltpu.VMEM((kk * tt, kk * tt), BF16),
            pltpu.VMEM((kk * tt, 2 * pp), BF16), pltpu.VMEM((kk * tt, 2 * pp), BF16),
            pltpu.VMEM((n_lvl, 2 * pp), F32), pltpu.VMEM((n_lvl, 2 * pp), F32)]
    prm = (a_re, a_im, log_dt, bt_re, bt_im, c_re, c_im)
    return pl.pallas_call(
        functools.partial(_s5_core_body, chunks_per_seq=chunks_per_seq),
        grid=(g + 1,),
        in_specs=halves + params(0) + params(1) + [
            pl.BlockSpec((1, kk, tt), lambda s: (applied(s), 0, 0))],
        out_specs=[pl.BlockSpec((None, flat_rows, tt), lambda s: (applied(s), 0, 0))] * n_half,
        out_shape=[jax.ShapeDtypeStruct((g, flat_rows, tt), F32)] * n_half,
        scratch_shapes=slot + slot + [pltpu.VMEM((rows, kk * tt), BF16)],
        compiler_params=pltpu.CompilerParams(
            dimension_semantics=("arbitrary",), vmem_limit_bytes=VMEM_BIG),
        name="s5_core",
    )(*([ht] * n_half), *prm, *prm, d_b)


def _s5_post_body(*refs):
    n_half = S5_GROUP // SUBLANES
    y_refs = refs[:n_half]
    x_ref, gt_ref, w_ref, o_ref, ys_ref = refs[n_half:]
    for r in range(S5_TM // S5_CHUNK):
        yt = jnp.concatenate(
            [y_refs[h][:, r * SUBLANES:(r + 1) * SUBLANES, :].reshape(S5_GROUPS * SUBLANES, S5_CHUNK)
             for h in range(n_half)], axis=0)
        ys_ref[r * S5_CHUNK:(r + 1) * S5_CHUNK, :] = yt.T.astype(BF16)
    yb = ys_ref[...]
    d = x_ref.shape[1]
    y1 = jnp.dot(yb, w_ref[:, :d], preferred_element_type=F32)
    y2 = jnp.dot(yb, w_ref[:, d:], preferred_element_type=F32)
    o_ref[...] = x_ref[...] + gt_ref[0] * (y1 * jax.nn.sigmoid(y2))


def _s5_post(yts, x2, mods3, gate_row, w_glu, tiles_per_batch):
    t, d = x2.shape
    rows = S5_TM // S5_CHUNK
    return pl.pallas_call(
        _s5_post_body,
        grid=(t // S5_TM,),
        in_specs=[pl.BlockSpec((S5_GROUPS, rows * SUBLANES, S5_CHUNK),
                               lambda i: (0, i, 0))] * (S5_GROUP // SUBLANES) + [
                  pl.BlockSpec((S5_TM, d), lambda i: (i, 0)),
                  _vec_spec(lambda i: gate_row(i // tiles_per_batch)),
                  _resident(w_glu.shape)],
        out_specs=pl.BlockSpec((S5_TM, d), lambda i: (i, 0)),
        out_shape=jax.ShapeDtypeStruct((t, d), F32),
        scratch_shapes=[pltpu.VMEM((S5_TM, d), BF16)],
        compiler_params=pltpu.CompilerParams(
            dimension_semantics=("arbitrary",), vmem_limit_bytes=VMEM_BIG),
        name="s5_post",
    )(*yts, x2, mods3, w_glu)


def kernel(x, c, positions, ada_w, ada_b, norm_g, ffn_w_in, ffn_w_out, ret_w_in, ret_w_out,
           s5_a_re, s5_a_im, s5_b_re, s5_b_im, s5_c_re, s5_c_im, s5_d, s5_log_dt, s5_w_glu,
           final_g):
    bsz, seq, d = x.shape
    depth = ada_w.shape[0]
    assert d == D_MODEL and seq % S5_TM == 0 and seq % RET_TM == 0 and RET_TM % RET_CHUNK == 0
    t = bsz * seq
    g, kk, pp = S5_GROUPS, S5_GROUP, S5_STATE

    mods3 = _ada(c, ada_w, ada_b).reshape(depth * bsz * N_MOD, 1, d)
    ng3 = norm_g.reshape(depth * 3, 1, d)
    x2 = x.reshape(t, d)
    pos2 = positions.reshape(t, 1)
    half = RET_QK_DIM // 2
    inv_freq = (ROPE_THETA ** (-jnp.arange(half, dtype=F32) / half)).reshape(1, half)
    ret_consts = _ret_consts()

    w_ffn = (ffn_w_in[0, 0].astype(BF16), ffn_w_out[0, 0].astype(BF16))
    for i in range(depth):
        mod_row = lambda s: (lambda b: (i * bsz + b) * N_MOD + 3 * s)
        gate_row = lambda s: (lambda b: (i * bsz + b) * N_MOD + 3 * s + 2)
        j = i // 2
        cast = [(ffn_w_in, (i, 1)), (ffn_w_out, (i, 1))]
        if i % 2 == 0:
            cast += [(ret_w_in, (j,)), (ret_w_out, (j,))]
        else:
            w_glu = s5_w_glu[j].reshape(g, kk // SUBLANES, SUBLANES, -1).transpose(1, 0, 2, 3)
            cast.append((w_glu.reshape(d, -1), ()))
        x2, done = _ffn(x2, ng3, mods3, i * 3, mod_row(0), *w_ffn, seq // FFN_TM, cast=cast)
        w_ffn = done[:2]
        if i % 2 == 0:
            x2 = _ret(x2, pos2, inv_freq, ng3, mods3, i * 3 + 1, mod_row(1), gate_row(1),
                      ret_consts, done[2], done[3], bsz, seq // RET_TM)
        else:
            ht = _s5_pre(x2, ng3, mods3, i * 3 + 1, mod_row(1), seq // S5_TM)
            yts = _s5_core(
                ht,
                s5_a_re[j].reshape(g, 1, pp), s5_a_im[j].reshape(g, 1, pp),
                s5_log_dt[j].reshape(g, 1, 1),
                s5_b_re[j].transpose(0, 2, 1), s5_b_im[j].transpose(0, 2, 1),
                s5_c_re[j], s5_c_im[j],
                jnp.broadcast_to(s5_d[j].reshape(g, kk, 1), (g, kk, S5_CHUNK)),
                seq // S5_CHUNK)
            x2 = _s5_post(yts, x2, mods3, gate_row(1), done[2], seq // S5_TM)
        last = i == depth - 1
        cast = [] if last else [(ffn_w_in, (i + 1, 0)), (ffn_w_out, (i + 1, 0))]
        x2, done = _ffn(x2, ng3, mods3, i * 3 + 2, mod_row(2), *w_ffn, seq // FFN_TM,
                        final_g=final_g if last else None, cast=cast)
        w_ffn = done[:2]
    return x2.reshape(bsz, seq, d)
```

```python
import functools

import jax
import jax.numpy as jnp
from jax import lax
from jax.experimental import pallas as pl
from jax.experimental.pallas import tpu as pltpu

F32 = jnp.float32
BF16 = jnp.bfloat16

D_MODEL = 1024
N_MOD = 9
D_FF = 2816
RET_HEADS = 4
RET_QK_DIM = 256
RET_V_DIM = 512
RET_CHUNK = 256
ROPE_THETA = 10000.0
S5_GROUP = 16
S5_GROUPS = 64
S5_STATE = 64
NORM_EPS = 1e-6
GN_EPS = 1e-5

LANES = 128
SUBLANES = 8
MXU_DIM = 256
S5_CHUNK = LANES
VMEM_BIG = 56 * 1024 * 1024
VMEM_MID = 40 * 1024 * 1024

FFN_TM = 1024
FFN_CHUNKS = ((0, 4 * MXU_DIM), (4 * MXU_DIM, 8 * MXU_DIM), (8 * MXU_DIM, D_FF))
assert D_FF % MXU_DIM == 0
RET_TM = 512
ADA_TN = 2304
S5_TM = 1024


def _rms_mod(x, g, shift, scale):
    y = x * lax.rsqrt(jnp.mean(x * x, axis=-1, keepdims=True) + NORM_EPS)
    return (y * g) * (1.0 + scale) + shift


def _cmul(ar, ai, br, bi):
    return ar * br - ai * bi, ar * bi + ai * br


def _resident(shape):
    zeros = (0,) * len(shape)
    return pl.BlockSpec(shape, lambda *_: zeros, pipeline_mode=pl.Buffered(1))


def _ada_body(ct_ref, w_ref, b_ref, o_ref):
    ct = ct_ref[...]
    ca = ct * jax.nn.sigmoid(ct)
    w = w_ref[0]
    for b in range(ct.shape[1]):
        o_ref[0, b:b + 1, :] = jnp.sum(w * ca[:, b:b + 1], axis=0, keepdims=True) + b_ref[0]


def _ada(c, ada_w, ada_b):
    depth, d, n = ada_w.shape
    bsz = c.shape[0]
    return pl.pallas_call(
        _ada_body,
        grid=(depth, n // ADA_TN),
        in_specs=[
            pl.BlockSpec((d, bsz), lambda i, j: (0, 0)),
            pl.BlockSpec((1, d, ADA_TN), lambda i, j: (i, 0, j)),
            pl.BlockSpec((1, 1, ADA_TN), lambda i, j: (i, 0, j)),
        ],
        out_specs=pl.BlockSpec((1, bsz, ADA_TN), lambda i, j: (i, 0, j)),
        out_shape=jax.ShapeDtypeStruct((depth, bsz, n), F32),
        compiler_params=pltpu.CompilerParams(
            dimension_semantics=("arbitrary", "arbitrary"), vmem_limit_bytes=VMEM_MID),
        name="ada",
    )(c.T, ada_w, ada_b.reshape(depth, 1, n))


def _ffn_body(x_ref, ng_ref, sh_ref, sc_ref, gt_ref, win_ref, wout_ref, *rest, final, n_cast):
    if final:
        fg_ref, rest = rest[0], rest[1:]
    src_refs, o_ref, dst_refs = rest[:n_cast], rest[n_cast], rest[n_cast + 1:]
    for src, dst in zip(src_refs, dst_refs):
        dst[...] = src[...].astype(BF16)
    x = x_ref[...]
    h = _rms_mod(x, ng_ref[0], sh_ref[0], sc_ref[0]).astype(BF16)
    acc = None
    for lo, hi in FFN_CHUNKS:
        g = jnp.dot(h, win_ref[:, lo:hi], preferred_element_type=F32)
        u = jnp.dot(h, win_ref[:, D_FF + lo:D_FF + hi], preferred_element_type=F32)
        a = (g * jax.nn.sigmoid(g) * u).astype(BF16)
        p = jnp.dot(a, wout_ref[lo:hi, :], preferred_element_type=F32)
        acc = p if acc is None else acc + p
    y = x + (0.5 * gt_ref[0]) * acc
    if final:
        y = (y * lax.rsqrt(jnp.mean(y * y, axis=-1, keepdims=True) + NORM_EPS)) * fg_ref[...]
    o_ref[...] = y


def _vec_spec(row_fn):
    return pl.BlockSpec((1, 1, D_MODEL), lambda *idx: (row_fn(*idx), 0, 0))


def _ffn(x2, ng3, mods3, norm_row, mod_row, w_in, w_out, tiles_per_batch, final_g=None, cast=()):
    t, d = x2.shape
    n_steps = t // FFN_TM
    final = final_g is not None
    in_specs = [
        pl.BlockSpec((FFN_TM, d), lambda i: (i, 0)),
        _vec_spec(lambda i: norm_row),
        _vec_spec(lambda i: mod_row(i // tiles_per_batch)),
        _vec_spec(lambda i: mod_row(i // tiles_per_batch) + 1),
        _vec_spec(lambda i: mod_row(i // tiles_per_batch) + 2),
        _resident(w_in.shape),
        _resident(w_out.shape),
    ]
    args = [x2, ng3, mods3, mods3, mods3, w_in, w_out]
    if final:
        in_specs.append(pl.BlockSpec((1, d), lambda i: (0, 0)))
        args.append(final_g.reshape(1, d))
    out_specs = [pl.BlockSpec((FFN_TM, d), lambda i: (i, 0))]
    out_shape = [jax.ShapeDtypeStruct((t, d), F32)]
    for arr, lead in cast:
        r, c = arr.shape[len(lead):]
        rb = r // n_steps
        assert rb * n_steps == r and rb % (2 * SUBLANES) == 0, (r, n_steps)
        in_specs.append(pl.BlockSpec((None,) * len(lead) + (rb, c),
                                     lambda i, lead=lead: tuple(lead) + (i, 0)))
        args.append(arr)
        out_specs.append(pl.BlockSpec((rb, c), lambda i: (i, 0)))
        out_shape.append(jax.ShapeDtypeStruct((r, c), BF16))
    outs = pl.pallas_call(
        functools.partial(_ffn_body, final=final, n_cast=len(cast)),
        grid=(n_steps,),
        in_specs=in_specs,
        out_specs=out_specs,
        out_shape=out_shape,
        compiler_params=pltpu.CompilerParams(
            dimension_semantics=("arbitrary",), vmem_limit_bytes=VMEM_BIG),
        name="ffn_final" if final else "ffn",
    )(*args)
    return outs[0], outs[1:]


def _ret_body(x_ref, pos_ref, invf_ref, ng_ref, sh_ref, sc_ref, gt_ref,
              mask_ref, qd_ref, kd_ref, cd_ref, w_ref, wout_ref, o_ref,
              q_ref, qx_ref, k_ref, kx_ref, v_ref, g_ref, gated_ref, state_ref):
    @pl.when(pl.program_id(1) == 0)
    def _():
        state_ref[...] = jnp.zeros_like(state_ref)

    x = x_ref[...]
    h = _rms_mod(x, ng_ref[0], sh_ref[0], sc_ref[0]).astype(BF16)
    half = RET_QK_DIM // 2
    qk_w = RET_HEADS * RET_QK_DIM
    v_w = RET_HEADS * RET_V_DIM
    ang = pos_ref[...].astype(F32) * invf_ref[...]
    cos = jnp.cos(ang)
    sin = jnp.sin(ang)
    for hh in range(RET_HEADS):
        qk = slice(hh * RET_QK_DIM, (hh + 1) * RET_QK_DIM)
        vv = slice(hh * RET_V_DIM, (hh + 1) * RET_V_DIM)
        v_ref[:, vv] = jnp.dot(h, w_ref[:, 2 * qk_w + vv.start:2 * qk_w + vv.stop],
                               preferred_element_type=F32).astype(BF16)
        gate = jnp.dot(h, w_ref[:, 2 * qk_w + v_w + vv.start:2 * qk_w + v_w + vv.stop],
                       preferred_element_type=F32)
        g_ref[:, vv] = (gate * jax.nn.sigmoid(gate)).astype(BF16)
        for base, dst, dstx, dec_ref, scale in ((0, q_ref, qx_ref, qd_ref, None),
                                                (qk_w, k_ref, kx_ref, kd_ref, RET_QK_DIM ** -0.5)):
            t = jnp.dot(h, w_ref[:, base + qk.start:base + qk.stop], preferred_element_type=F32)
            t1 = t[:, :half]
            t2 = t[:, half:]
            r1 = t1 * cos - t2 * sin
            r2 = t1 * sin + t2 * cos
            if scale is not None:
                r1 = r1 * scale
                r2 = r2 * scale
            dst[:, qk.start:qk.start + half] = r1.astype(BF16)
            dst[:, qk.start + half:qk.stop] = r2.astype(BF16)
            dec = dec_ref[hh]
            dstx[:, qk.start:qk.start + half] = (r1 * dec).astype(BF16)
            dstx[:, qk.start + half:qk.stop] = (r2 * dec).astype(BF16)

        for c in range(RET_TM // RET_CHUNK):
            rows = slice(c * RET_CHUNK, (c + 1) * RET_CHUNK)
            vh = v_ref[rows, vv]
            st = state_ref[hh]
            scores = lax.dot_general(q_ref[rows, qk], k_ref[rows, qk], (((1,), (1,)), ((), ())),
                                     preferred_element_type=F32)
            update = lax.dot_general(kx_ref[rows, qk], vh, (((0,), (0,)), ((), ())),
                                     preferred_element_type=F32)
            cross = jnp.dot(qx_ref[rows, qk], st.astype(BF16), preferred_element_type=F32)
            state_ref[hh] = st * cd_ref[hh] + update
            inner = jnp.dot((scores * mask_ref[hh]).astype(BF16), vh, preferred_element_type=F32)
            o = inner + cross
            mu = jnp.mean(o, axis=-1, keepdims=True)
            oc = o - mu
            var = jnp.mean(oc * oc, axis=-1, keepdims=True)
            on = oc * lax.rsqrt(var + GN_EPS)
            gated_ref[rows, vv] = g_ref[rows, vv] * on.astype(BF16)
    acc = jnp.dot(gated_ref[...], wout_ref[...], preferred_element_type=F32)
    o_ref[...] = x + gt_ref[0] * acc


def _ret(x2, pos2, inv_freq, ng3, mods3, norm_row, mod_row, gate_row, consts, w_in, w_out,
         bsz, tiles_per_batch):
    t, d = x2.shape
    qk_w = RET_HEADS * RET_QK_DIM
    v_w = RET_HEADS * RET_V_DIM
    mask, qd, kd, cd = consts
    tok = lambda w: pl.BlockSpec((RET_TM, w), lambda b, n: (b * tiles_per_batch + n, 0))
    full = lambda a: pl.BlockSpec(a.shape, lambda b, n: (0,) * a.ndim)
    return pl.pallas_call(
        _ret_body,
        grid=(bsz, tiles_per_batch),
        in_specs=[
            tok(d),
            tok(1),
            pl.BlockSpec((1, RET_QK_DIM // 2), lambda b, n: (0, 0)),
            _vec_spec(lambda b, n: norm_row),
            _vec_spec(lambda b, n: mod_row(b)),
            _vec_spec(lambda b, n: mod_row(b) + 1),
            _vec_spec(lambda b, n: gate_row(b)),
            full(mask), full(qd), full(kd), full(cd),
            _resident(w_in.shape),
            _resident(w_out.shape),
        ],
        out_specs=tok(d),
        out_shape=jax.ShapeDtypeStruct((t, d), F32),
        scratch_shapes=[pltpu.VMEM((RET_TM, qk_w), BF16), pltpu.VMEM((RET_TM, qk_w), BF16),
                        pltpu.VMEM((RET_TM, qk_w), BF16), pltpu.VMEM((RET_TM, qk_w), BF16),
                        pltpu.VMEM((RET_TM, v_w), BF16), pltpu.VMEM((RET_TM, v_w), BF16),
                        pltpu.VMEM((RET_TM, v_w), BF16),
                        pltpu.VMEM((RET_HEADS, RET_QK_DIM, RET_V_DIM), F32)],
        compiler_params=pltpu.CompilerParams(
            dimension_semantics=("arbitrary", "arbitrary"), vmem_limit_bytes=VMEM_BIG),
        name="ret",
    )(x2, pos2, inv_freq, ng3, mods3, mods3, mods3, mask, qd, kd, cd, w_in, w_out)


def _ret_consts():
    hds, c = RET_HEADS, RET_CHUNK
    log_gamma = jnp.log(1.0 - jnp.power(2.0, -5.0 - jnp.arange(hds, dtype=F32)))
    pos = jnp.arange(c, dtype=F32)
    diff = pos[:, None] - pos[None, :]
    mask = jnp.where(diff >= 0, jnp.exp(log_gamma[:, None, None] * jnp.maximum(diff, 0.0)), 0.0)
    reps = (1, RET_TM // c, 1)
    qd = jnp.tile(jnp.exp(log_gamma[:, None] * (pos + 1.0))[:, :, None], reps)
    kd = jnp.tile(jnp.exp(log_gamma[:, None] * (c - 1.0 - pos))[:, :, None], reps)
    cd = jnp.broadcast_to(jnp.exp(log_gamma * c)[:, None, None], (hds, 1, RET_V_DIM))
    return mask, qd, kd, cd


def _s5_pre_body(x_ref, ng_ref, sh_ref, sc_ref, o_ref):
    h = _rms_mod(x_ref[...], ng_ref[0], sh_ref[0], sc_ref[0])
    d = h.shape[1]
    for r in range(S5_TM // S5_CHUNK):
        ht = h[r * S5_CHUNK:(r + 1) * S5_CHUNK, :].T
        o_ref[:, r * SUBLANES:(r + 1) * SUBLANES, :] = ht.reshape(d // SUBLANES, SUBLANES, S5_CHUNK)


def _s5_pre(x2, ng3, mods3, norm_row, mod_row, tiles_per_batch):
    t, d = x2.shape
    rows = S5_TM // S5_CHUNK * SUBLANES
    return pl.pallas_call(
        _s5_pre_body,
        grid=(t // S5_TM,),
        in_specs=[pl.BlockSpec((S5_TM, d), lambda i: (i, 0)),
                  _vec_spec(lambda i: norm_row),
                  _vec_spec(lambda i: mod_row(i // tiles_per_batch)),
                  _vec_spec(lambda i: mod_row(i // tiles_per_batch) + 1)],
        out_specs=pl.BlockSpec((d // SUBLANES, rows, S5_CHUNK), lambda i: (0, i, 0)),
        out_shape=jax.ShapeDtypeStruct((d // SUBLANES, t // S5_CHUNK * SUBLANES, S5_CHUNK), F32),
        compiler_params=pltpu.CompilerParams(
            dimension_semantics=("arbitrary",), vmem_limit_bytes=VMEM_MID),
        name="s5_pre",
    )(x2, ng3, mods3, mods3)


def _pair(re, im):
    return jnp.concatenate([re, im], axis=1)


def _cmul2(x, xs, w1, w2):
    return x * w1 + xs * w2, xs * w1 - x * w2


def _s5_levels(chunks_per_seq):
    return S5_CHUNK.bit_length() - 1, (chunks_per_seq - 1).bit_length()


def _s5_terms(are_ref, aim_ref, ldt_ref, btr_ref, bti_ref, cr_ref, ci_ref, chunks_per_seq):
    pp = S5_STATE
    a_re = are_ref[0]
    a_im = aim_ref[0]
    dt = jnp.exp(ldt_ref[0])
    lam_re = dt * a_re
    lam_im = dt * a_im
    mag = jnp.exp(lam_re)
    n_tab, n_scan = _s5_levels(chunks_per_seq)
    n_lvl = n_tab + n_scan
    sq = [(mag * jnp.cos(lam_im), mag * jnp.sin(lam_im))]
    for _ in range(n_lvl - 1):
        sq.append(_cmul(sq[-1][0], sq[-1][1], sq[-1][0], sq[-1][1]))
    lvl_re = jnp.concatenate([r for r, _ in sq], axis=0)
    lvl_im = jnp.concatenate([i for _, i in sq], axis=0)
    w1_all = _pair(lvl_re, lvl_re)
    w2_all = _pair(-lvl_im, lvl_im)
    w1 = lambda b: w1_all[b:b + 1, :]
    w2 = lambda b: w2_all[b:b + 1, :]

    def power_table(first, ascending):
        lane = lax.broadcasted_iota(jnp.int32, (SUBLANES, 2 * pp), 1)
        x = jnp.where(lane < pp, 1.0, 0.0).astype(F32)
        xs = 1.0 - x
        for b in range(SUBLANES.bit_length()):
            nx, nxs = _cmul2(x, xs, w1(b), w2(b))
            bit = ((first >> b) & 1) == 1
            x = jnp.where(bit, nx, x)
            xs = jnp.where(bit, nxs, xs)
        for b in range(SUBLANES.bit_length() - 1, n_tab):
            nx, nxs = _cmul2(x, xs, w1(b), w2(b))
            x = jnp.concatenate([x, nx] if ascending else [nx, x], axis=0)
            xs = jnp.concatenate([xs, nxs] if ascending else [nxs, xs], axis=0)
        return x, xs

    nr, ni = sq[0][0] - 1.0, sq[0][1]
    den = a_re * a_re + a_im * a_im
    q_re = (nr * a_re + ni * a_im) / den
    q_im = (ni * a_re - nr * a_im) / den
    bbr, bbi = _cmul(q_re, q_im, btr_ref[0], bti_ref[0])
    bb = _pair(bbr, bbi)
    bbs = _pair(bbi, bbr)
    b1 = _pair(bbr, bbr)
    b2 = _pair(-bbi, bbi)
    c_re = cr_ref[0]
    c_im = ci_ref[0]
    c1 = _pair(c_re, -c_re)
    c2 = _pair(-c_im, -c_im)
    return dict(w1_all=w1_all, w2_all=w2_all, power_table=power_table,
                bb=bb, bbs=bbs, b1=b1, b2=b2, c1=c1, c2=c2)


def _s5_lags(*refs, chunks_per_seq):
    *param_refs, kt_ref = refs
    kk, pp = S5_GROUP, S5_STATE
    tm = _s5_terms(*param_refs, chunks_per_seq)
    sub = lax.broadcasted_iota(jnp.int32, (SUBLANES, 2 * pp), 0)
    e0, _ = tm["power_table"](sub, True)
    cb = jnp.concatenate([tm["bb"] * tm["c1"][k:k + 1, :] + tm["bbs"] * tm["c2"][k:k + 1, :]
                          for k in range(kk)], axis=0)
    kt_ref[...] = lax.dot_general(cb, e0, (((1,), (1,)), ((), ())),
                                  precision=lax.Precision.HIGHEST, preferred_element_type=F32)


def _s5_build(*refs, chunks_per_seq):
    *param_refs, kt_ref, toep_ref, ws_ref, wo_ref, w1_ref, w2_ref = refs
    kk, pp, tt = S5_GROUP, S5_STATE, S5_CHUNK
    tm = _s5_terms(*param_refs, chunks_per_seq)
    b1, b2, c1, c2 = tm["b1"], tm["b2"], tm["c1"], tm["c2"]
    sub = lax.broadcasted_iota(jnp.int32, (SUBLANES, 2 * pp), 0)
    e1, e1s = tm["power_table"](sub + 1, True)
    er, ers = tm["power_table"](SUBLANES - 1 - sub, False)

    th = tt // 2
    causal = (lax.broadcasted_iota(jnp.int32, (th, tt), 1)
              >= lax.broadcasted_iota(jnp.int32, (th, tt), 0))
    for kp in range(kk):
        for k in range(kk):
            lag = kt_ref[k * kk + kp:k * kk + kp + 1, :]
            blk = pltpu.roll(jnp.broadcast_to(lag, (th, tt)), 0, 1, stride=1, stride_axis=0)
            toep_ref[kp * th:(kp + 1) * th, k * tt:(k + 1) * tt] = (
                jnp.where(causal, blk, 0.0).astype(BF16))
    for kp in range(kk):
        w = (er * b1[kp:kp + 1, :] + ers * b2[kp:kp + 1, :]).astype(BF16)
        for half in range(2):
            ws_ref[half * kk * th + kp * th:half * kk * th + (kp + 1) * th, :] = (
                w[half * th:(half + 1) * th, :])
    for k in range(kk):
        wo_ref[k * tt:(k + 1) * tt, :] = (e1 * c1[k:k + 1, :] + e1s * c2[k:k + 1, :]).astype(BF16)
    w1_ref[...] = tm["w1_all"]
    w2_ref[...] = tm["w2_all"]


def _s5_apply(u_refs, d_ref, o_refs, toep_ref, ws_ref, wo_ref, w1_ref, w2_ref, ucat_ref, *,
              chunks_per_seq):
    kk, pp, tt = S5_GROUP, S5_STATE, S5_CHUNK
    n_tab, n_scan = _s5_levels(chunks_per_seq)
    n_rows = u_refs[0].shape[0] // SUBLANES

    def chan(block_refs, k):
        return block_refs[k // SUBLANES], pl.ds(k % SUBLANES, n_rows, stride=SUBLANES)

    us = []
    for k in range(kk):
        flat, rows_k = chan(u_refs, k)
        us.append(flat[rows_k, :])
    th = tt // 2
    first_half = lax.broadcasted_iota(jnp.int32, (n_rows, tt), 1) < th
    for pair in range(kk // 2):
        a, b = us[2 * pair], us[2 * pair + 1]
        lo = jnp.where(first_half, a, pltpu.roll(b, th, 1))
        hi = jnp.where(first_half, pltpu.roll(a, th, 1), b)
        ucat_ref[0, :, pair * tt:(pair + 1) * tt] = lo.astype(BF16)
        ucat_ref[1, :, pair * tt:(pair + 1) * tt] = hi.astype(BF16)

    n_k = kk * th
    s = (jnp.dot(ucat_ref[0], ws_ref[:n_k, :], preferred_element_type=F32)
         + jnp.dot(ucat_ref[1], ws_ref[n_k:, :], preferred_element_type=F32))
    ss = pltpu.roll(s, pp, 1)
    cidx = lax.broadcasted_iota(jnp.int32, (n_rows, 2 * pp), 0) % chunks_per_seq
    for lvl in range(n_scan):
        sh = 1 << lvl
        row = slice(n_tab + lvl, n_tab + lvl + 1)
        tx, txs = _cmul2(pltpu.roll(s, sh, 0), pltpu.roll(ss, sh, 0), w1_ref[row, :], w2_ref[row, :])
        keep = cidx >= sh
        s = s + jnp.where(keep, tx, 0.0)
        ss = ss + jnp.where(keep, txs, 0.0)
    prev = jnp.where(cidx >= 1, pltpu.roll(s, 1, 0), 0.0).astype(BF16)

    z2 = jnp.dot(ucat_ref[...].reshape(2 * n_rows, n_k), toep_ref[...],
                 preferred_element_type=F32)
    inter = lax.dot_general(prev, wo_ref[...], (((1,), (1,)), ((), ())),
                            preferred_element_type=F32)
    for k in range(kk):
        cols = slice(k * tt, (k + 1) * tt)
        late = jnp.where(first_half, 0.0, pltpu.roll(z2[n_rows:, cols], th, 1))
        z = z2[:n_rows, cols] + late + inter[:, cols] + d_ref[0, k:k + 1, :] * us[k]
        flat, rows_k = chan(o_refs, k)
        flat[rows_k, :] = jax.nn.gelu(z)


def _s5_core_body(*refs, chunks_per_seq):
    n_half = S5_GROUP // SUBLANES
    n_par = 7
    u_refs = refs[:n_half]
    cur_refs = refs[n_half:n_half + n_par]
    nxt_refs = refs[n_half + n_par:n_half + 2 * n_par]
    d_ref = refs[n_half + 2 * n_par]
    rest = refs[n_half + 2 * n_par + 1:]
    o_refs, rest = rest[:n_half], rest[n_half:]
    slots = (rest[0:6], rest[6:12])
    ucat_ref = rest[12]
    step = pl.program_id(0)

    @pl.when(step == 0)
    def _():
        for ref in slots[1][1:]:
            ref[...] = jnp.zeros_like(ref)
        _s5_lags(*cur_refs, slots[0][0], chunks_per_seq=chunks_per_seq)

    for parity in range(2):
        @pl.when(step % 2 == parity)
        def _():
            _s5_build(*cur_refs, *slots[parity], chunks_per_seq=chunks_per_seq)
            _s5_apply(u_refs, d_ref, o_refs, *slots[1 - parity][1:], ucat_ref,
                      chunks_per_seq=chunks_per_seq)
            _s5_lags(*nxt_refs, slots[1 - parity][0], chunks_per_seq=chunks_per_seq)


def _s5_core(ht, a_re, a_im, log_dt, bt_re, bt_im, c_re, c_im, d_b, chunks_per_seq):
    _, flat_rows, tt = ht.shape
    rows = flat_rows // SUBLANES
    g, kk, pp = S5_GROUPS, S5_GROUP, S5_STATE
    n_lvl = sum(_s5_levels(chunks_per_seq))
    applied = lambda s: jnp.maximum(s - 1, 0)

    def params(ahead):
        group = lambda s: jnp.minimum(s + ahead, g - 1)
        par = lambda *shape: pl.BlockSpec((1,) + shape, lambda s: (group(s),) + (0,) * len(shape))
        return [par(1, pp), par(1, pp), par(1, 1), par(kk, pp), par(kk, pp), par(kk, pp), par(kk, pp)]

    n_half = kk // SUBLANES
    halves = [pl.BlockSpec((None, flat_rows, tt), lambda s, h=h: (applied(s) * n_half + h, 0, 0))
              for h in range(n_half)]
    slot = [pltpu.VMEM((kk * kk, tt), F32),
            pltpu.VMEM((kk * tt // 2, kk * tt), BF16),
            pltpu.VMEM((kk * tt, 2 * pp), BF16), pltpu.VMEM((kk * tt, 2 * pp), BF16),
            pltpu.VMEM((n_lvl, 2 * pp), F32), pltpu.VMEM((n_lvl, 2 * pp), F32)]
    prm = (a_re, a_im, log_dt, bt_re, bt_im, c_re, c_im)
    return pl.pallas_call(
        functools.partial(_s5_core_body, chunks_per_seq=chunks_per_seq),
        grid=(g + 1,),
        in_specs=halves + params(0) + params(1) + [
            pl.BlockSpec((1, kk, tt), lambda s: (applied(s), 0, 0))],
        out_specs=[pl.BlockSpec((None, flat_rows, tt), lambda s: (applied(s), 0, 0))] * n_half,
        out_shape=[jax.ShapeDtypeStruct((g, flat_rows, tt), F32)] * n_half,
        scratch_shapes=slot + slot + [pltpu.VMEM((2, rows, kk * tt // 2), BF16)],
        compiler_params=pltpu.CompilerParams(
            dimension_semantics=("arbitrary",), vmem_limit_bytes=VMEM_BIG),
        name="s5_core",
    )(*([ht] * n_half), *prm, *prm, d_b)


def _s5_post_body(*refs):
    n_half = S5_GROUP // SUBLANES
    y_refs = refs[:n_half]
    x_ref, gt_ref, w_ref, o_ref, ys_ref = refs[n_half:]
    for r in range(S5_TM // S5_CHUNK):
        yt = jnp.concatenate(
            [y_refs[h][:, r * SUBLANES:(r + 1) * SUBLANES, :].reshape(S5_GROUPS * SUBLANES, S5_CHUNK)
             for h in range(n_half)], axis=0)
        ys_ref[r * S5_CHUNK:(r + 1) * S5_CHUNK, :] = yt.T.astype(BF16)
    yb = ys_ref[...]
    d = x_ref.shape[1]
    y1 = jnp.dot(yb, w_ref[:, :d], preferred_element_type=F32)
    y2 = jnp.dot(yb, w_ref[:, d:], preferred_element_type=F32)
    o_ref[...] = x_ref[...] + gt_ref[0] * (y1 * jax.nn.sigmoid(y2))


def _s5_post(yts, x2, mods3, gate_row, w_glu, tiles_per_batch):
    t, d = x2.shape
    rows = S5_TM // S5_CHUNK
    return pl.pallas_call(
        _s5_post_body,
        grid=(t // S5_TM,),
        in_specs=[pl.BlockSpec((S5_GROUPS, rows * SUBLANES, S5_CHUNK),
                               lambda i: (0, i, 0))] * (S5_GROUP // SUBLANES) + [
                  pl.BlockSpec((S5_TM, d), lambda i: (i, 0)),
                  _vec_spec(lambda i: gate_row(i // tiles_per_batch)),
                  _resident(w_glu.shape)],
        out_specs=pl.BlockSpec((S5_TM, d), lambda i: (i, 0)),
        out_shape=jax.ShapeDtypeStruct((t, d), F32),
        scratch_shapes=[pltpu.VMEM((S5_TM, d), BF16)],
        compiler_params=pltpu.CompilerParams(
            dimension_semantics=("arbitrary",), vmem_limit_bytes=VMEM_BIG),
        name="s5_post",
    )(*yts, x2, mods3, w_glu)


def kernel(x, c, positions, ada_w, ada_b, norm_g, ffn_w_in, ffn_w_out, ret_w_in, ret_w_out,
           s5_a_re, s5_a_im, s5_b_re, s5_b_im, s5_c_re, s5_c_im, s5_d, s5_log_dt, s5_w_glu,
           final_g):
    bsz, seq, d = x.shape
    depth = ada_w.shape[0]
    assert d == D_MODEL and seq % S5_TM == 0 and seq % RET_TM == 0 and RET_TM % RET_CHUNK == 0
    t = bsz * seq
    g, kk, pp = S5_GROUPS, S5_GROUP, S5_STATE

    mods3 = _ada(c, ada_w, ada_b).reshape(depth * bsz * N_MOD, 1, d)
    ng3 = norm_g.reshape(depth * 3, 1, d)
    x2 = x.reshape(t, d)
    pos2 = positions.reshape(t, 1)
    half = RET_QK_DIM // 2
    inv_freq = (ROPE_THETA ** (-jnp.arange(half, dtype=F32) / half)).reshape(1, half)
    ret_consts = _ret_consts()

    w_ffn = (ffn_w_in[0, 0].astype(BF16), ffn_w_out[0, 0].astype(BF16))
    for i in range(depth):
        mod_row = lambda s: (lambda b: (i * bsz + b) * N_MOD + 3 * s)
        gate_row = lambda s: (lambda b: (i * bsz + b) * N_MOD + 3 * s + 2)
        j = i // 2
        cast = [(ffn_w_in, (i, 1)), (ffn_w_out, (i, 1))]
        if i % 2 == 0:
            cast += [(ret_w_in, (j,)), (ret_w_out, (j,))]
        else:
            w_glu = s5_w_glu[j].reshape(g, kk // SUBLANES, SUBLANES, -1).transpose(1, 0, 2, 3)
            cast.append((w_glu.reshape(d, -1), ()))
        x2, done = _ffn(x2, ng3, mods3, i * 3, mod_row(0), *w_ffn, seq // FFN_TM, cast=cast)
        w_ffn = done[:2]
        if i % 2 == 0:
            x2 = _ret(x2, pos2, inv_freq, ng3, mods3, i * 3 + 1, mod_row(1), gate_row(1),
                      ret_consts, done[2], done[3], bsz, seq // RET_TM)
        else:
            ht = _s5_pre(x2, ng3, mods3, i * 3 + 1, mod_row(1), seq // S5_TM)
            yts = _s5_core(
                ht,
                s5_a_re[j].reshape(g, 1, pp), s5_a_im[j].reshape(g, 1, pp),
                s5_log_dt[j].reshape(g, 1, 1),
                s5_b_re[j].transpose(0, 2, 1), s5_b_im[j].transpose(0, 2, 1),
                s5_c_re[j], s5_c_im[j],
                jnp.broadcast_to(s5_d[j].reshape(g, kk, 1), (g, kk, S5_CHUNK)),
                seq // S5_CHUNK)
            x2 = _s5_post(yts, x2, mods3, gate_row(1), done[2], seq // S5_TM)
        last = i == depth - 1
        cast = [] if last else [(ffn_w_in, (i + 1, 0)), (ffn_w_out, (i + 1, 0))]
        x2, done = _ffn(x2, ng3, mods3, i * 3 + 2, mod_row(2), *w_ffn, seq // FFN_TM,
                        final_g=final_g if last else None, cast=cast)
        w_ffn = done[:2]
    return x2.reshape(bsz, seq, d)
```
